```python
import math
import jax, jax.numpy as jnp
from jax import lax
import numpy as np

D_MODEL = 2048
BATCH = 4
SEQ = 4096
DEPTH = 4

M_HEADS = 8
M_QK_DIM = 128
M_V_DIM = 256
M_CHUNK = 64
CONV_K = 4
M_QK = M_HEADS * M_QK_DIM
M_V = M_HEADS * M_V_DIM
D_HEADS = 8
D_QK_DIM = 128
D_V_DIM = 2 * D_QK_DIM
D_Q = D_HEADS * 2 * D_QK_DIM
D_V = D_HEADS * D_V_DIM
Q_BLOCK = 128
D_FF = -(-8 * D_MODEL // 768) * 256
EPS = 1e-6
IN_SIZES = (M_QK, M_QK, M_V, M_V, M_HEADS, M_HEADS, D_Q, D_Q, D_V, D_MODEL, D_MODEL)
N_IN = sum(IN_SIZES)

kernel_name = "hybrid_mlstm_diffattn_gated_block"


def rmsnorm(x, w):
    xf = x.astype(jnp.float32)
    y = xf * lax.rsqrt(jnp.mean(xf * xf, axis=-1, keepdims=True) + EPS)
    return (y * w.astype(jnp.float32)).astype(x.dtype)


def head_rmsnorm(x, w):
    B, S, H, d = x.shape
    xf = x.astype(jnp.float32)
    y = xf * lax.rsqrt(jnp.mean(xf * xf, axis=-1, keepdims=True) + EPS)
    return y.reshape(B, S, H * d) * w.astype(jnp.float32)


def split_in(proj):
    outs, off = [], 0
    for n in IN_SIZES:
        outs.append(proj[..., off:off + n])
        off += n
    return outs


def causal_dwconv(u, w, b):
    C = u.shape[-1]
    y = lax.conv_general_dilated(u, w[:, None, :].astype(u.dtype), window_strides=(1,),
                                 padding=[(CONV_K - 1, 0)],
                                 dimension_numbers=('NWC', 'WIO', 'NWC'),
                                 feature_group_count=C)
    return y + b.astype(u.dtype)


def mlstm_chunkwise(q, k, v, ig, lf):
    B, S, H, dk = q.shape
    dv = v.shape[-1]
    nc = S // M_CHUNK

    def chunks(t):
        t = t.reshape((B, nc, M_CHUNK) + t.shape[2:])
        return t.transpose((1, 0, 3, 2) + tuple(range(4, t.ndim)))

    tril = jnp.tril(jnp.ones((M_CHUNK, M_CHUNK), dtype=bool))

    def step(carry, xs):
        C, n, m = carry
        qc, kc, vc, igc, lfc = xs
        b = jnp.cumsum(lfc, axis=-1)
        log_d = jnp.where(tril, b[..., :, None] - b[..., None, :] + igc[..., None, :], -jnp.inf)
        log_inter = b + m[..., None]
        m_row = jnp.maximum(log_inter, jnp.max(log_d, axis=-1))
        dmat = jnp.exp(log_d - m_row[..., None])
        inter = jnp.exp(log_inter - m_row)
        s = jnp.einsum('bhld,bhsd->bhls', qc, kc) * dmat
        num = jnp.einsum('bhls,bhsv->bhlv', s, vc) + inter[..., None] * jnp.einsum('bhvd,bhld->bhlv', C, qc)
        den = jnp.sum(s, axis=-1) + inter * jnp.einsum('bhd,bhld->bhl', n, qc)
        h = num / jnp.maximum(jnp.abs(den), jnp.exp(-m_row))[..., None]
        bL = b[..., -1]
        log_w = bL[..., None] - b + igc
        m_new = jnp.maximum(bL + m, jnp.max(log_w, axis=-1))
        w = jnp.exp(log_w - m_new[..., None])
        decay = jnp.exp(bL + m - m_new)
        C_new = decay[..., None, None] * C + jnp.einsum('bhl,bhlv,bhld->bhvd', w, vc, kc)
        n_new = decay[..., None] * n + jnp.einsum('bhl,bhld->bhd', w, kc)
        return (C_new, n_new, m_new), h

    init = (jnp.zeros((B, H, dv, dk), jnp.float32), jnp.zeros((B, H, dk), jnp.float32),
            jnp.zeros((B, H), jnp.float32))
    _, hs = lax.scan(step, init, (chunks(q), chunks(k), chunks(v), chunks(ig), chunks(lf)))
    return hs.transpose(1, 0, 3, 2, 4).reshape(B, S, H, dv)


def diff_attention(q, k, v, lam):
    B, S, H, _, d = q.shape
    dv = v.shape[-1]
    nb = S // Q_BLOCK
    kt = k.transpose(0, 2, 3, 1, 4)
    vt = v.transpose(0, 2, 1, 3)
    qb = (q * (d ** -0.5)).reshape(B, nb, Q_BLOCK, H, 2, d).transpose(1, 0, 3, 4, 2, 5)
    kpos = jnp.arange(S)

    def block(args):
        qblk, start = args
        s = jnp.einsum('bhcqd,bhckd->bhcqk', qblk, kt).astype(jnp.float32)
        qpos = start + jnp.arange(Q_BLOCK)
        mask = kpos[None, :] <= qpos[:, None]
        p = jax.nn.softmax(jnp.where(mask, s, -jnp.inf), axis=-1)
        a = p[:, :, 0] - lam * p[:, :, 1]
        return jnp.einsum('bhqk,bhkv->bhqv', a.astype(v.dtype), vt)

    out = lax.map(block, (qb, jnp.arange(nb, dtype=jnp.int32) * Q_BLOCK))
    return out.transpose(1, 0, 3, 2, 4).reshape(B, S, H, dv)


def hybrid_mixer(h, w_in, conv_w, conv_b, b_ig, b_fg, m_norm_w, lq1, lk1, lq2, lk2,
                 d_norm_w, w_bm, w_bd, w_out, lam_init):
    B, S, _ = h.shape
    proj = h @ w_in
    mq, mk, mv, mo, mi, mf, dq, dk, dv, gm, gd = split_in(proj)
    qk = jax.nn.silu(causal_dwconv(jnp.concatenate([mq, mk], axis=-1), conv_w, conv_b))
    q = qk[..., :M_QK].reshape(B, S, M_HEADS, M_QK_DIM).astype(jnp.float32)
    k = qk[..., M_QK:].reshape(B, S, M_HEADS, M_QK_DIM).astype(jnp.float32) * (M_QK_DIM ** -0.5)
    v = mv.reshape(B, S, M_HEADS, M_V_DIM).astype(jnp.float32)
    ig = (mi + b_ig).astype(jnp.float32)
    lf = jax.nn.log_sigmoid((mf + b_fg).astype(jnp.float32))
    hm = mlstm_chunkwise(q, k, v, ig, lf)
    hm = jax.nn.sigmoid(mo) * head_rmsnorm(hm, m_norm_w).astype(h.dtype)
    lam = (jnp.exp(jnp.sum(lq1.astype(jnp.float32) * lk1.astype(jnp.float32)))
           - jnp.exp(jnp.sum(lq2.astype(jnp.float32) * lk2.astype(jnp.float32))) + lam_init)
    hd = diff_attention(dq.reshape(B, S, D_HEADS, 2, D_QK_DIM), dk.reshape(B, S, D_HEADS, 2, D_QK_DIM),
                        dv.reshape(B, S, D_HEADS, D_V_DIM), lam)
    hd = (head_rmsnorm(hd, d_norm_w) * (1.0 - lam_init)).astype(h.dtype)
    y = jax.nn.sigmoid(gm) * (hm @ w_bm) + jax.nn.sigmoid(gd) * (hd @ w_bd)
    return y @ w_out


def swiglu(h, w_gate, w_up, w_down):
    return (jax.nn.silu(h @ w_gate) * (h @ w_up)) @ w_down


def setup_inputs(seed: int = 0) -> dict:
    key = jax.random.key(seed)
    ks = jax.random.split(key, 24)
    nrm = lambda k, shape, s: jax.random.normal(k, shape, jnp.float32) * s
    L = DEPTH
    f_bias = jnp.linspace(3.0, 6.0, M_HEADS, dtype=jnp.float32)[None, :] + nrm(ks[6], (L, M_HEADS), 0.1)
    return {
        "x": nrm(ks[0], (BATCH, SEQ, D_MODEL), 1.0),
        "attn_norm_w": 1.0 + nrm(ks[1], (L, D_MODEL), 0.02),
        "w_in": nrm(ks[2], (L, D_MODEL, N_IN), D_MODEL ** -0.5),
        "conv_w": nrm(ks[3], (L, CONV_K, 2 * M_QK), CONV_K ** -0.5),
        "conv_b": nrm(ks[4], (L, 2 * M_QK), 0.01),
        "b_igate": nrm(ks[5], (L, M_HEADS), 0.1),
        "b_fgate": f_bias,
        "mlstm_norm_w": 1.0 + nrm(ks[7], (L, M_V), 0.02),
        "lambda_q1": nrm(ks[8], (L, D_QK_DIM), 0.1),
        "lambda_k1": nrm(ks[9], (L, D_QK_DIM), 0.1),
        "lambda_q2": nrm(ks[10], (L, D_QK_DIM), 0.1),
        "lambda_k2": nrm(ks[11], (L, D_QK_DIM), 0.1),
        "diff_norm_w": 1.0 + nrm(ks[12], (L, D_V), 0.02),
        "w_branch_m": nrm(ks[13], (L, M_V, D_MODEL), M_V ** -0.5),
        "w_branch_d": nrm(ks[14], (L, D_V, D_MODEL), D_V ** -0.5),
        "w_out": nrm(ks[15], (L, D_MODEL, D_MODEL), D_MODEL ** -0.5),
        "ffn_norm_w": 1.0 + nrm(ks[16], (L, D_MODEL), 0.02),
        "w_ffn_gate": nrm(ks[17], (L, D_MODEL, D_FF), D_MODEL ** -0.5),
        "w_ffn_up": nrm(ks[18], (L, D_MODEL, D_FF), D_MODEL ** -0.5),
        "w_ffn_down": nrm(ks[19], (L, D_FF, D_MODEL), D_FF ** -0.5),
        "final_norm_w": 1.0 + nrm(ks[20], (D_MODEL,), 0.02),
    }


def reference(x, attn_norm_w, w_in, conv_w, conv_b, b_igate, b_fgate, mlstm_norm_w,
              lambda_q1, lambda_k1, lambda_q2, lambda_k2, diff_norm_w, w_branch_m, w_branch_d,
              w_out, ffn_norm_w, w_ffn_gate, w_ffn_up, w_ffn_down, final_norm_w):
    for l in range(DEPTH):
        lam_init = 0.8 - 0.6 * math.exp(-0.3 * l)
        h = rmsnorm(x, attn_norm_w[l])
        x = x + hybrid_mixer(h, w_in[l], conv_w[l], conv_b[l], b_igate[l], b_fgate[l], mlstm_norm_w[l],
                             lambda_q1[l], lambda_k1[l], lambda_q2[l], lambda_k2[l], diff_norm_w[l],
                             w_branch_m[l], w_branch_d[l], w_out[l], lam_init)
        h = rmsnorm(x, ffn_norm_w[l])
        x = x + swiglu(h, w_ffn_gate[l], w_ffn_up[l], w_ffn_down[l])
    return rmsnorm(x, final_norm_w)
```

```python
import functools
import math

import jax
import jax.numpy as jnp
from jax import lax
from jax.experimental import pallas as pl
from jax.experimental.pallas import tpu as pltpu

D_MODEL = 2048
DEPTH = 4
M_HEADS = 8
M_QK_DIM = 128
M_V_DIM = 256
CONV_K = 4
M_QK = M_HEADS * M_QK_DIM
M_V = M_HEADS * M_V_DIM
D_HEADS = 8
D_QK_DIM = 128
D_V_DIM = 256
D_Q = D_HEADS * 2 * D_QK_DIM
D_V = D_HEADS * D_V_DIM
D_FF = 5632
EPS = 1e-6

IN_GATE_LO = 2 * M_QK + 2 * M_V
IN_GATE_HI = IN_GATE_LO + 2 * M_HEADS
OFF_MQ = 0
OFF_MK = M_QK
OFF_MV = 2 * M_QK
OFF_MO = OFF_MV + M_V
OFF_DQ = OFF_MO + M_V
OFF_DK = OFF_DQ + D_Q
OFF_DV = OFF_DK + D_Q
OFF_GM = OFF_DV + D_V
OFF_GD = OFF_GM + D_MODEL
N_PROJ = OFF_GD + D_MODEL

LANES = 128
SUBLANES = 8
M_CHUNK = 256
GATE_ROWS = 1024
VMEM_LIMIT = 56 * 1024 * 1024

_F32 = jnp.float32
_BF16 = jnp.bfloat16


def _dot(a, b):
    return jnp.dot(a, b, preferred_element_type=_F32)


def _dot_nt(a, b):
    return lax.dot_general(a, b, (((1,), (1,)), ((), ())), preferred_element_type=_F32)


def _dot_tn(a, b):
    return lax.dot_general(a, b, (((0,), (0,)), ((), ())), preferred_element_type=_F32)


def _sigmoid(x):
    return 1.0 / (1.0 + jnp.exp(-x))


def _rmsnorm_rows(x, w):
    ms = jnp.mean(x * x, axis=-1, keepdims=True)
    return x * lax.rsqrt(ms + EPS) * w


def _params(sem):
    return pltpu.CompilerParams(dimension_semantics=sem, vmem_limit_bytes=VMEM_LIMIT)


def _inproj_kernel(x_ref, nw_ref, w_ref, wg_ref, o_ref, g_ref, h_scr):
    @pl.when(pl.program_id(1) == 0)
    def _():
        hb = _rmsnorm_rows(x_ref[...], nw_ref[...]).astype(_BF16)
        h_scr[...] = hb
        g_ref[...] = _dot(hb, wg_ref[...])

    o_ref[...] = _dot(h_scr[...], w_ref[...]).astype(o_ref.dtype)


def _inproj(x, nw, w, wg, tm=1024, tn=1024):
    T = x.shape[0]
    return pl.pallas_call(
        _inproj_kernel,
        grid=(T // tm, N_PROJ // tn),
        in_specs=[
            pl.BlockSpec((tm, D_MODEL), lambda i, j: (i, 0)),
            pl.BlockSpec((1, D_MODEL), lambda i, j: (0, 0)),
            pl.BlockSpec((D_MODEL, tn), lambda i, j: (0, j)),
            pl.BlockSpec((D_MODEL, LANES), lambda i, j: (0, 0)),
        ],
        out_specs=[
            pl.BlockSpec((tm, tn), lambda i, j: (i, j)),
            pl.BlockSpec((tm, LANES), lambda i, j: (i, 0)),
        ],
        out_shape=[
            jax.ShapeDtypeStruct((T, N_PROJ), _BF16),
            jax.ShapeDtypeStruct((T, LANES), _F32),
        ],
        scratch_shapes=[pltpu.VMEM((tm, D_MODEL), _BF16)],
        compiler_params=_params(("parallel", "arbitrary")),
        name="inproj",
    )(x, nw, w, wg)


def _gates_kernel(g_ref, b_ref, gcol_ref, grow_ref):
    z = g_ref[...] + b_ref[...]
    rows = z.shape[0]
    lane = lax.broadcasted_iota(jnp.int32, z.shape, 1)
    lf = jnp.minimum(z, 0.0) - jnp.log1p(jnp.exp(-jnp.abs(z)))
    pos = lax.broadcasted_iota(jnp.int32, z.shape, 0) % M_CHUNK
    csum = lf
    d = 1
    while d < M_CHUNK:
        csum = csum + jnp.where(pos >= d, pltpu.roll(csum, d, 0), 0.0)
        d *= 2
    is_f = (lane >= M_HEADS) & (lane < 2 * M_HEADS)
    col = jnp.where(is_f, csum, z)
    gcol_ref[...] = col
    tr = col.T
    grow_ref[...] = tr[0:M_HEADS, :] - tr[M_HEADS:2 * M_HEADS, :]


def _gates(gpre, bias):
    T = gpre.shape[0]
    return pl.pallas_call(
        _gates_kernel,
        grid=(T // GATE_ROWS,),
        in_specs=[
            pl.BlockSpec((GATE_ROWS, LANES), lambda i: (i, 0)),
            pl.BlockSpec((1, LANES), lambda i: (0, 0)),
        ],
        out_specs=[
            pl.BlockSpec((GATE_ROWS, LANES), lambda i: (i, 0)),
            pl.BlockSpec((M_HEADS, GATE_ROWS), lambda i: (0, i)),
        ],
        out_shape=[
            jax.ShapeDtypeStruct((T, LANES), _F32),
            jax.ShapeDtypeStruct((M_HEADS, T), _F32),
        ],
        compiler_params=_params(("parallel",)),
        name="gates",
    )(gpre, bias)


def _mlstm_kernel(q_ref, k_ref, v_ref, o_ref, cwq_ref, cbq_ref, cwk_ref, cbk_ref,
                  gcol_ref, grow_ref, nw_ref, out_ref, xq_scr, xk_scr, c_scr, m_scr):
    head = pl.program_id(1)
    S = q_ref.shape[0]
    L = M_CHUNK
    pad = SUBLANES
    zpad = jnp.zeros((pad, M_QK_DIM), _F32)
    xq_scr[0:pad, :] = zpad
    xk_scr[0:pad, :] = zpad
    xq_scr[pad:, :] = q_ref[...].astype(_F32)
    xk_scr[pad:, :] = k_ref[...].astype(_F32)
    c_scr[...] = jnp.zeros(c_scr.shape, _F32)
    m_scr[...] = jnp.zeros(m_scr.shape, _F32)

    lane = lax.broadcasted_iota(jnp.int32, (L, LANES), 1)
    row_i = lax.broadcasted_iota(jnp.int32, (L, L), 0)
    col_i = lax.broadcasted_iota(jnp.int32, (L, L), 1)
    tril = col_i <= row_i
    ones_aug = jnp.ones((L, LANES), _BF16)
    k_scale = M_QK_DIM ** -0.5

    def conv_silu(x_scr, cw_ref, cb_ref, r0):
        win = x_scr[pl.ds(r0, L + pad), :]
        acc = cb_ref[...] + cw_ref[CONV_K - 1:CONV_K, :] * win[pad:, :]
        for j in range(CONV_K - 1):
            acc = acc + cw_ref[j:j + 1, :] * pltpu.roll(win, CONV_K - 1 - j, 0)[pad:, :]
        return acc * _sigmoid(acc)

    def chunk(c, carry):
        r0 = pl.multiple_of(c * L, L)
        qf = conv_silu(xq_scr, cwq_ref, cbq_ref, r0)
        kf = conv_silu(xk_scr, cwk_ref, cbk_ref, r0) * k_scale
        qb = qf.astype(_BF16)
        kb = kf.astype(_BF16)
        vaug = jnp.concatenate([v_ref[pl.ds(r0, L), :], ones_aug], axis=1)

        g = gcol_ref[pl.ds(r0, L), :]
        ig = jnp.sum(jnp.where(lane == head, g, 0.0), axis=-1, keepdims=True)
        b = jnp.sum(jnp.where(lane == head + M_HEADS, g, 0.0), axis=-1, keepdims=True)
        r = grow_ref[pl.ds(head, 1), pl.ds(r0, L)]
        m_prev = m_scr[...]

        log_d = jnp.where(tril, b + r, -jnp.inf)
        log_inter = b + m_prev
        m_row = jnp.maximum(log_inter, jnp.max(log_d, axis=-1, keepdims=True))
        dmat = jnp.exp(log_d - m_row)
        inter = jnp.exp(log_inter - m_row)
        s = _dot_nt(qb, kb) * dmat
        num_aug = _dot(s.astype(_BF16), vaug) + inter * _dot(qb, c_scr[...].astype(_BF16))
        num = num_aug[:, :M_V_DIM]
        den = num_aug[:, M_V_DIM:M_V_DIM + 1]
        hh = num / jnp.maximum(jnp.abs(den), jnp.exp(-m_row))
        y = _rmsnorm_rows(hh, nw_ref[...])
        gate = _sigmoid(o_ref[pl.ds(r0, L), :].astype(_F32))
        out_ref[pl.ds(r0, L), :] = (gate * y).astype(out_ref.dtype)

        b_last = b[L - 1:L, :]
        log_w = b_last - b + ig
        m_new = jnp.maximum(b_last + m_prev, jnp.max(log_w, axis=0, keepdims=True))
        w = jnp.exp(log_w - m_new)
        decay = jnp.exp(b_last + m_prev - m_new)
        kw = (kf * w).astype(_BF16)
        c_scr[...] = decay * c_scr[...] + _dot_tn(kw, vaug)
        m_scr[...] = m_new
        return carry

    lax.fori_loop(0, S // L, chunk, 0)


def _mlstm(proj3, conv_w, conv_b, gcol3, grow, nw):
    B, S, _ = proj3.shape
    qk_blk = lambda off: (lambda b, h: (b, 0, off // M_QK_DIM + h))
    v_blk = lambda off: (lambda b, h: (b, 0, off // M_V_DIM + h))
    return pl.pallas_call(
        _mlstm_kernel,
        grid=(B, M_HEADS),
        in_specs=[
            pl.BlockSpec((None, S, M_QK_DIM), qk_blk(OFF_MQ)),
            pl.BlockSpec((None, S, M_QK_DIM), qk_blk(OFF_MK)),
            pl.BlockSpec((None, S, M_V_DIM), v_blk(OFF_MV)),
            pl.BlockSpec((None, S, M_V_DIM), v_blk(OFF_MO)),
            pl.BlockSpec((CONV_K, M_QK_DIM), lambda b, h: (0, h)),
            pl.BlockSpec((1, M_QK_DIM), lambda b, h: (0, h)),
            pl.BlockSpec((CONV_K, M_QK_DIM), lambda b, h: (0, M_HEADS + h)),
            pl.BlockSpec((1, M_QK_DIM), lambda b, h: (0, M_HEADS + h)),
            pl.BlockSpec((None, S, LANES), lambda b, h: (b, 0, 0)),
            pl.BlockSpec((M_HEADS, S), lambda b, h: (0, b)),
            pl.BlockSpec((1, M_V_DIM), lambda b, h: (0, h)),
        ],
        out_specs=pl.BlockSpec((None, S, M_V_DIM), lambda b, h: (b, 0, h)),
        out_shape=jax.ShapeDtypeStruct((B, S, M_V), _BF16),
        scratch_shapes=[
            pltpu.VMEM((S + SUBLANES, M_QK_DIM), _F32),
            pltpu.VMEM((S + SUBLANES, M_QK_DIM), _F32),
            pltpu.VMEM((M_QK_DIM, M_V_DIM + LANES), _F32),
            pltpu.VMEM((1, 1), _F32),
        ],
        compiler_params=_params(("parallel", "parallel")),
        name="mlstm",
    )(proj3, proj3, proj3, proj3, conv_w, conv_b, conv_w, conv_b, gcol3, grow, nw)


def _attn_kernel(lq1_ref, lk1_ref, lq2_ref, lk2_ref, q_ref, k_ref, v_ref, nw_ref, o_ref,
                 acc1, acc2, m1, l1, m2, l2, *, lam_init, tile):
    qi = pl.program_id(2)
    d = D_QK_DIM
    c = (d ** -0.5) * math.log2(math.e)
    q = q_ref[...]
    qs = (q[:, :d], q[:, d:])
    streams = ((acc1, m1, l1), (acc2, m2, l2))
    for acc, m, l in streams:
        acc[...] = jnp.zeros(acc.shape, _F32)
        m[...] = jnp.full(m.shape, -jnp.inf, _F32)
        l[...] = jnp.zeros(l.shape, _F32)

    row_i = lax.broadcasted_iota(jnp.int32, (tile, tile), 0)
    col_i = lax.broadcasted_iota(jnp.int32, (tile, tile), 1)
    causal = col_i <= row_i

    def step(j, masked):
        r0 = pl.multiple_of(j * tile, tile)
        kk = k_ref[pl.ds(r0, tile), :]
        vv = v_ref[pl.ds(r0, tile), :]
        for idx, (acc, m, l) in enumerate(streams):
            s = _dot_nt(qs[idx], kk[:, idx * d:(idx + 1) * d])
            if masked:
                s = jnp.where(causal, s, -jnp.inf)
            m_prev = m[...]
            m_new = jnp.maximum(m_prev, jnp.max(s, axis=-1, keepdims=True))
            p = jnp.exp2((s - m_new) * c)
            alpha = jnp.exp2((m_prev - m_new) * c)
            l[...] = alpha * l[...] + jnp.sum(p, axis=-1, keepdims=True)
            acc[...] = alpha * acc[...] + _dot(p.astype(_BF16), vv)
            m[...] = m_new

    def body(j, carry):
        step(j, False)
        return carry

    lax.fori_loop(0, qi, body, 0)
    step(qi, True)

    lam = (jnp.exp(jnp.sum(lq1_ref[...] * lk1_ref[...], axis=-1, keepdims=True))
           - jnp.exp(jnp.sum(lq2_ref[...] * lk2_ref[...], axis=-1, keepdims=True)) + lam_init)
    hd = acc1[...] / l1[...] - lam * (acc2[...] / l2[...])
    y = _rmsnorm_rows(hd, nw_ref[...]) * (1.0 - lam_init)
    o_ref[...] = y.astype(o_ref.dtype)


def _attn(proj3, lq1, lk1, lq2, lk2, nw, lam_init, tile=512):
    B, S, _ = proj3.shape
    w2 = 2 * D_QK_DIM
    lam_spec = pl.BlockSpec((1, D_QK_DIM), lambda b, h, i: (0, 0))
    return pl.pallas_call(
        functools.partial(_attn_kernel, lam_init=lam_init, tile=tile),
        grid=(B, D_HEADS, S // tile),
        in_specs=[
            lam_spec, lam_spec, lam_spec, lam_spec,
            pl.BlockSpec((None, tile, w2), lambda b, h, i: (b, i, OFF_DQ // w2 + h)),
            pl.BlockSpec((None, S, w2), lambda b, h, i: (b, 0, OFF_DK // w2 + h)),
            pl.BlockSpec((None, S, D_V_DIM), lambda b, h, i: (b, 0, OFF_DV // D_V_DIM + h)),
            pl.BlockSpec((1, D_V_DIM), lambda b, h, i: (0, h)),
        ],
        out_specs=pl.BlockSpec((None, tile, D_V_DIM), lambda b, h, i: (b, i, h)),
        out_shape=jax.ShapeDtypeStruct((B, S, D_V), _BF16),
        scratch_shapes=[
            pltpu.VMEM((tile, D_V_DIM), _F32),
            pltpu.VMEM((tile, D_V_DIM), _F32),
            pltpu.VMEM((tile, 1), _F32),
            pltpu.VMEM((tile, 1), _F32),
            pltpu.VMEM((tile, 1), _F32),
            pltpu.VMEM((tile, 1), _F32),
        ],
        compiler_params=_params(("parallel", "parallel", "arbitrary")),
        name="diffattn",
    )(lq1, lk1, lq2, lk2, proj3, proj3, proj3, nw)


def _merge_kernel(hm_ref, hd_ref, wm_ref, wd_ref, gm_ref, gd_ref, y_ref):
    a = _dot(hm_ref[...], wm_ref[...])
    b = _dot(hd_ref[...], wd_ref[...])
    y = _sigmoid(gm_ref[...].astype(_F32)) * a + _sigmoid(gd_ref[...].astype(_F32)) * b
    y_ref[...] = y.astype(y_ref.dtype)


def _merge(hm, hd, wm, wd, proj, tm=1024, tn=1024):
    T = hm.shape[0]
    return pl.pallas_call(
        _merge_kernel,
        grid=(T // tm, D_MODEL // tn),
        in_specs=[
            pl.BlockSpec((tm, M_V), lambda i, j: (i, 0)),
            pl.BlockSpec((tm, D_V), lambda i, j: (i, 0)),
            pl.BlockSpec((M_V, tn), lambda i, j: (0, j)),
            pl.BlockSpec((D_V, tn), lambda i, j: (0, j)),
            pl.BlockSpec((tm, tn), lambda i, j: (i, OFF_GM // tn + j)),
            pl.BlockSpec((tm, tn), lambda i, j: (i, OFF_GD // tn + j)),
        ],
        out_specs=pl.BlockSpec((tm, tn), lambda i, j: (i, j)),
        out_shape=jax.ShapeDtypeStruct((T, D_MODEL), _BF16),
        compiler_params=_params(("parallel", "parallel")),
        name="merge",
    )(hm, hd, wm, wd, proj, proj)


def _resmm_kernel(a_ref, w_ref, x_ref, o_ref):
    o_ref[...] = x_ref[...] + _dot(a_ref[...], w_ref[...])


def _resmm(a, w, x, tm, tn, name):
    T, K = a.shape
    N = w.shape[1]
    return pl.pallas_call(
        _resmm_kernel,
        grid=(T // tm, N // tn),
        in_specs=[
            pl.BlockSpec((tm, K), lambda i, j: (i, 0)),
            pl.BlockSpec((K, tn), lambda i, j: (0, j)),
            pl.BlockSpec((tm, tn), lambda i, j: (i, j)),
        ],
        out_specs=pl.BlockSpec((tm, tn), lambda i, j: (i, j)),
        out_shape=jax.ShapeDtypeStruct((T, N), _F32),
        compiler_params=_params(("parallel", "parallel")),
        name=name,
    )(a, w, x)


def _ffn_up_kernel(x_ref, nw_ref, wg_ref, wu_ref, a_ref, h_scr):
    @pl.when(pl.program_id(1) == 0)
    def _():
        h_scr[...] = _rmsnorm_rows(x_ref[...], nw_ref[...]).astype(_BF16)

    h = h_scr[...]
    g = _dot(h, wg_ref[...])
    u = _dot(h, wu_ref[...])
    a_ref[...] = (g * _sigmoid(g) * u).astype(a_ref.dtype)


def _ffn_up(x, nw, wg, wu, tm=1024, tn=512):
    T = x.shape[0]
    return pl.pallas_call(
        _ffn_up_kernel,
        grid=(T // tm, D_FF // tn),
        in_specs=[
            pl.BlockSpec((tm, D_MODEL), lambda i, j: (i, 0)),
            pl.BlockSpec((1, D_MODEL), lambda i, j: (0, 0)),
            pl.BlockSpec((D_MODEL, tn), lambda i, j: (0, j)),
            pl.BlockSpec((D_MODEL, tn), lambda i, j: (0, j)),
        ],
        out_specs=pl.BlockSpec((tm, tn), lambda i, j: (i, j)),
        out_shape=jax.ShapeDtypeStruct((T, D_FF), _BF16),
        scratch_shapes=[pltpu.VMEM((tm, D_MODEL), _BF16)],
        compiler_params=_params(("parallel", "arbitrary")),
        name="ffn_up",
    )(x, nw, wg, wu)


def _norm_kernel(x_ref, nw_ref, o_ref):
    o_ref[...] = _rmsnorm_rows(x_ref[...], nw_ref[...])


def _final_norm(x, nw, tm=512):
    T = x.shape[0]
    return pl.pallas_call(
        _norm_kernel,
        grid=(T // tm,),
        in_specs=[
            pl.BlockSpec((tm, D_MODEL), lambda i: (i, 0)),
            pl.BlockSpec((1, D_MODEL), lambda i: (0, 0)),
        ],
        out_specs=pl.BlockSpec((tm, D_MODEL), lambda i: (i, 0)),
        out_shape=jax.ShapeDtypeStruct((T, D_MODEL), _F32),
        compiler_params=_params(("parallel",)),
        name="final_norm",
    )(x, nw)


def kernel(x, attn_norm_w, w_in, conv_w, conv_b, b_igate, b_fgate, mlstm_norm_w, lambda_q1, lambda_k1, lambda_q2, lambda_k2, diff_norm_w, w_branch_m, w_branch_d, w_out, ffn_norm_w, w_ffn_gate, w_ffn_up, w_ffn_down, final_norm_w):
    B, S, D = x.shape
    T = B * S
    xf = x.reshape(T, D)
    gate_pad = jnp.zeros((LANES - 2 * M_HEADS,), _F32)
    for l in range(DEPTH):
        lam_init = 0.8 - 0.6 * math.exp(-0.3 * l)
        wl = w_in[l]
        w_proj = jnp.concatenate([wl[:, :IN_GATE_LO], wl[:, IN_GATE_HI:]], axis=1).astype(_BF16)
        w_gate = jnp.pad(wl[:, IN_GATE_LO:IN_GATE_HI], ((0, 0), (0, LANES - 2 * M_HEADS))).astype(_BF16)
        gate_bias = jnp.concatenate([b_igate[l], b_fgate[l], gate_pad]).reshape(1, LANES)

        proj, gpre = _inproj(xf, attn_norm_w[l].reshape(1, D), w_proj, w_gate)
        gcol, grow = _gates(gpre, gate_bias)
        proj3 = proj.reshape(B, S, N_PROJ)
        hm = _mlstm(proj3, conv_w[l], conv_b[l].reshape(1, 2 * M_QK), gcol.reshape(B, S, LANES), grow,
                    mlstm_norm_w[l].reshape(1, M_V))
        hd = _attn(proj3, lambda_q1[l].reshape(1, D_QK_DIM), lambda_k1[l].reshape(1, D_QK_DIM),
                   lambda_q2[l].reshape(1, D_QK_DIM), lambda_k2[l].reshape(1, D_QK_DIM),
                   diff_norm_w[l].reshape(1, D_V), lam_init)
        y = _merge(hm.reshape(T, M_V), hd.reshape(T, D_V), w_branch_m[l].astype(_BF16),
                   w_branch_d[l].astype(_BF16), proj)
        xf = _resmm(y, w_out[l].astype(_BF16), xf, 1024, 1024, "out_proj")
        a = _ffn_up(xf, ffn_norm_w[l].reshape(1, D), w_ffn_gate[l].astype(_BF16), w_ffn_up[l].astype(_BF16))
        xf = _resmm(a, w_ffn_down[l].astype(_BF16), xf, 1024, 512, "ffn_down")
    out = _final_norm(xf, final_norm_w.reshape(1, D))
    return out.reshape(B, S, D)
```

```python
import functools
import math

import jax
import jax.numpy as jnp
from jax import lax
from jax.experimental import pallas as pl
from jax.experimental.pallas import tpu as pltpu

D_MODEL = 2048
DEPTH = 4
M_HEADS = 8
M_QK_DIM = 128
M_V_DIM = 256
CONV_K = 4
M_QK = M_HEADS * M_QK_DIM
M_V = M_HEADS * M_V_DIM
D_HEADS = 8
D_QK_DIM = 128
D_V_DIM = 256
D_Q = D_HEADS * 2 * D_QK_DIM
D_V = D_HEADS * D_V_DIM
D_FF = 5632
EPS = 1e-6

IN_GATE_LO = 2 * M_QK + 2 * M_V
IN_GATE_HI = IN_GATE_LO + 2 * M_HEADS
OFF_MQ = 0
OFF_MK = M_QK
OFF_MV = 2 * M_QK
OFF_MO = OFF_MV + M_V
OFF_DQ = OFF_MO + M_V
OFF_DK = OFF_DQ + D_Q
OFF_DV = OFF_DK + D_Q
OFF_GM = OFF_DV + D_V
OFF_GD = OFF_GM + D_MODEL
N_PROJ = OFF_GD + D_MODEL

LANES = 128
SUBLANES = 8
M_CHUNK = 256
GATE_ROWS = 1024
VMEM_LIMIT = 56 * 1024 * 1024

_F32 = jnp.float32
_BF16 = jnp.bfloat16


def _dot(a, b):
    return jnp.dot(a, b, preferred_element_type=_F32)


def _dot_nt(a, b):
    return lax.dot_general(a, b, (((1,), (1,)), ((), ())), preferred_element_type=_F32)


def _dot_tn(a, b):
    return lax.dot_general(a, b, (((0,), (0,)), ((), ())), preferred_element_type=_F32)


def _sigmoid(x):
    return 1.0 / (1.0 + jnp.exp(-x))


def _rmsnorm_rows(x, w):
    ms = jnp.mean(x * x, axis=-1, keepdims=True)
    return x * lax.rsqrt(ms + EPS) * w


def _params(sem):
    return pltpu.CompilerParams(dimension_semantics=sem, vmem_limit_bytes=VMEM_LIMIT)


def _inproj_kernel(x_ref, nw_ref, w_ref, wg_ref, o_ref, g_ref, h_scr):
    @pl.when(pl.program_id(1) == 0)
    def _():
        hb = _rmsnorm_rows(x_ref[...], nw_ref[...]).astype(_BF16)
        h_scr[...] = hb
        g_ref[...] = _dot(hb, wg_ref[...])

    o_ref[...] = _dot(h_scr[...], w_ref[...]).astype(o_ref.dtype)


def _inproj(x, nw, w, wg, tm=1024, tn=1024):
    T = x.shape[0]
    return pl.pallas_call(
        _inproj_kernel,
        grid=(T // tm, N_PROJ // tn),
        in_specs=[
            pl.BlockSpec((tm, D_MODEL), lambda i, j: (i, 0)),
            pl.BlockSpec((1, D_MODEL), lambda i, j: (0, 0)),
            pl.BlockSpec((D_MODEL, tn), lambda i, j: (0, j)),
            pl.BlockSpec((D_MODEL, LANES), lambda i, j: (0, 0)),
        ],
        out_specs=[
            pl.BlockSpec((tm, tn), lambda i, j: (i, j)),
            pl.BlockSpec((tm, LANES), lambda i, j: (i, 0)),
        ],
        out_shape=[
            jax.ShapeDtypeStruct((T, N_PROJ), _BF16),
            jax.ShapeDtypeStruct((T, LANES), _F32),
        ],
        scratch_shapes=[pltpu.VMEM((tm, D_MODEL), _BF16)],
        compiler_params=_params(("parallel", "arbitrary")),
        name="inproj",
    )(x, nw, w, wg)


def _gates_kernel(g_ref, b_ref, gcol_ref, grow_ref):
    z = g_ref[...] + b_ref[...]
    rows = z.shape[0]
    lane = lax.broadcasted_iota(jnp.int32, z.shape, 1)
    lf = jnp.minimum(z, 0.0) - jnp.log1p(jnp.exp(-jnp.abs(z)))
    pos = lax.broadcasted_iota(jnp.int32, z.shape, 0) % M_CHUNK
    csum = lf
    d = 1
    while d < M_CHUNK:
        csum = csum + jnp.where(pos >= d, pltpu.roll(csum, d, 0), 0.0)
        d *= 2
    is_f = (lane >= M_HEADS) & (lane < 2 * M_HEADS)
    col = jnp.where(is_f, csum, z)
    gcol_ref[...] = col
    tr = col.T
    grow_ref[...] = tr[0:M_HEADS, :] - tr[M_HEADS:2 * M_HEADS, :]


def _gates(gpre, bias):
    T = gpre.shape[0]
    return pl.pallas_call(
        _gates_kernel,
        grid=(T // GATE_ROWS,),
        in_specs=[
            pl.BlockSpec((GATE_ROWS, LANES), lambda i: (i, 0)),
            pl.BlockSpec((1, LANES), lambda i: (0, 0)),
        ],
        out_specs=[
            pl.BlockSpec((GATE_ROWS, LANES), lambda i: (i, 0)),
            pl.BlockSpec((M_HEADS, GATE_ROWS), lambda i: (0, i)),
        ],
        out_shape=[
            jax.ShapeDtypeStruct((T, LANES), _F32),
            jax.ShapeDtypeStruct((M_HEADS, T), _F32),
        ],
        compiler_params=_params(("parallel",)),
        name="gates",
    )(gpre, bias)


def _mlstm_kernel(q_ref, k_ref, v_ref, o_ref, cwq_ref, cbq_ref, cwk_ref, cbk_ref,
                  gcol_ref, grow_ref, nw_ref, out_ref, xq_scr, xk_scr, c_scr, m_scr):
    head = pl.program_id(1)
    S = q_ref.shape[0]
    L = M_CHUNK
    pad = SUBLANES
    zpad = jnp.zeros((pad, M_QK_DIM), _F32)
    xq_scr[0:pad, :] = zpad
    xk_scr[0:pad, :] = zpad
    xq_scr[pad:, :] = q_ref[...].astype(_F32)
    xk_scr[pad:, :] = k_ref[...].astype(_F32)
    c_scr[...] = jnp.zeros(c_scr.shape, _F32)
    m_scr[...] = jnp.zeros(m_scr.shape, _F32)

    lane = lax.broadcasted_iota(jnp.int32, (L, LANES), 1)
    row_i = lax.broadcasted_iota(jnp.int32, (L, L), 0)
    col_i = lax.broadcasted_iota(jnp.int32, (L, L), 1)
    tril = col_i <= row_i
    ones_aug = jnp.ones((L, LANES), _BF16)
    k_scale = M_QK_DIM ** -0.5

    def conv_silu(x_scr, cw_ref, cb_ref, r0):
        win = x_scr[pl.ds(r0, L + pad), :]
        acc = cb_ref[...] + cw_ref[CONV_K - 1:CONV_K, :] * win[pad:, :]
        for j in range(CONV_K - 1):
            acc = acc + cw_ref[j:j + 1, :] * pltpu.roll(win, CONV_K - 1 - j, 0)[pad:, :]
        return acc * _sigmoid(acc)

    def chunk(c, carry):
        r0 = pl.multiple_of(c * L, L)
        qf = conv_silu(xq_scr, cwq_ref, cbq_ref, r0)
        kf = conv_silu(xk_scr, cwk_ref, cbk_ref, r0) * k_scale
        qb = qf.astype(_BF16)
        kb = kf.astype(_BF16)
        vaug = jnp.concatenate([v_ref[pl.ds(r0, L), :], ones_aug], axis=1)

        g = gcol_ref[pl.ds(r0, L), :]
        ig = jnp.sum(jnp.where(lane == head, g, 0.0), axis=-1, keepdims=True)
        b = jnp.sum(jnp.where(lane == head + M_HEADS, g, 0.0), axis=-1, keepdims=True)
        r = grow_ref[pl.ds(head, 1), pl.ds(r0, L)]
        m_prev = m_scr[...]

        log_d = jnp.where(tril, b + r, -jnp.inf)
        log_inter = b + m_prev
        m_row = jnp.maximum(log_inter, jnp.max(log_d, axis=-1, keepdims=True))
        dmat = jnp.exp(log_d - m_row)
        inter = jnp.exp(log_inter - m_row)
        s = _dot_nt(qb, kb) * dmat
        num_aug = _dot(s.astype(_BF16), vaug) + inter * _dot(qb, c_scr[...].astype(_BF16))
        num = num_aug[:, :M_V_DIM]
        den = num_aug[:, M_V_DIM:M_V_DIM + 1]
        hh = num / jnp.maximum(jnp.abs(den), jnp.exp(-m_row))
        y = _rmsnorm_rows(hh, nw_ref[...])
        gate = _sigmoid(o_ref[pl.ds(r0, L), :].astype(_F32))
        out_ref[pl.ds(r0, L), :] = (gate * y).astype(out_ref.dtype)

        b_last = b[L - 1:L, :]
        log_w = b_last - b + ig
        m_new = jnp.maximum(b_last + m_prev, jnp.max(log_w, axis=0, keepdims=True))
        w = jnp.exp(log_w - m_new)
        decay = jnp.exp(b_last + m_prev - m_new)
        kw = (kf * w).astype(_BF16)
        c_scr[...] = decay * c_scr[...] + _dot_tn(kw, vaug)
        m_scr[...] = m_new
        return carry

    lax.fori_loop(0, S // L, chunk, 0)


def _mlstm(proj3, conv_w, conv_b, gcol3, grow, nw):
    B, S, _ = proj3.shape
    qk_blk = lambda off: (lambda b, h: (b, 0, off // M_QK_DIM + h))
    v_blk = lambda off: (lambda b, h: (b, 0, off // M_V_DIM + h))
    return pl.pallas_call(
        _mlstm_kernel,
        grid=(B, M_HEADS),
        in_specs=[
            pl.BlockSpec((None, S, M_QK_DIM), qk_blk(OFF_MQ)),
            pl.BlockSpec((None, S, M_QK_DIM), qk_blk(OFF_MK)),
            pl.BlockSpec((None, S, M_V_DIM), v_blk(OFF_MV)),
            pl.BlockSpec((None, S, M_V_DIM), v_blk(OFF_MO)),
            pl.BlockSpec((CONV_K, M_QK_DIM), lambda b, h: (0, h)),
            pl.BlockSpec((1, M_QK_DIM), lambda b, h: (0, h)),
            pl.BlockSpec((CONV_K, M_QK_DIM), lambda b, h: (0, M_HEADS + h)),
            pl.BlockSpec((1, M_QK_DIM), lambda b, h: (0, M_HEADS + h)),
            pl.BlockSpec((None, S, LANES), lambda b, h: (b, 0, 0)),
            pl.BlockSpec((M_HEADS, S), lambda b, h: (0, b)),
            pl.BlockSpec((1, M_V_DIM), lambda b, h: (0, h)),
        ],
        out_specs=pl.BlockSpec((None, S, M_V_DIM), lambda b, h: (b, 0, h)),
        out_shape=jax.ShapeDtypeStruct((B, S, M_V), _BF16),
        scratch_shapes=[
            pltpu.VMEM((S + SUBLANES, M_QK_DIM), _F32),
            pltpu.VMEM((S + SUBLANES, M_QK_DIM), _F32),
            pltpu.VMEM((M_QK_DIM, M_V_DIM + LANES), _F32),
            pltpu.VMEM((1, 1), _F32),
        ],
        compiler_params=_params(("parallel", "parallel")),
        name="mlstm",
    )(proj3, proj3, proj3, proj3, conv_w, conv_b, conv_w, conv_b, gcol3, grow, nw)


def _attn_kernel(lq1_ref, lk1_ref, lq2_ref, lk2_ref, q_ref, k_ref, v_ref, nw_ref, o_ref,
                 acc1, acc2, m1, l1, m2, l2, *, lam_init, tile, sub):
    qi = pl.program_id(2)
    d = D_QK_DIM
    c = (d ** -0.5) * math.log2(math.e)
    n_lane_blk = tile // LANES
    streams = ((acc1, m1, l1), (acc2, m2, l2))
    for acc, m, l in streams:
        acc[...] = jnp.zeros(acc.shape, _F32)
        m[...] = jnp.full(m.shape, -jnp.inf, _F32)
        l[...] = jnp.zeros(l.shape, _F32)

    row_i = lax.broadcasted_iota(jnp.int32, (sub, LANES), 0)
    col_i = lax.broadcasted_iota(jnp.int32, (sub, LANES), 1)

    def step(j, masked):
        r0 = pl.multiple_of(j * tile, tile)
        for idx, (acc, m, l) in enumerate(streams):
            kk = k_ref[pl.ds(r0, tile), idx * d:(idx + 1) * d]
            vv = v_ref[pl.ds(r0, tile), :]
            for r in range(tile // sub):
                rows = pl.ds(r * sub, sub)
                s = _dot_nt(q_ref[rows, idx * d:(idx + 1) * d], kk)
                blocks = [s[:, b * LANES:(b + 1) * LANES] for b in range(n_lane_blk)]
                if masked:
                    blocks = [blk if (b + 1) * LANES - 1 <= r * sub else
                              jnp.where(col_i + b * LANES <= row_i + r * sub, blk, -jnp.inf)
                              for b, blk in enumerate(blocks)]
                m_cur = blocks[0]
                for blk in blocks[1:]:
                    m_cur = jnp.maximum(m_cur, blk)
                m_prev = m[rows, :]
                m_new = jnp.maximum(m_prev, jnp.max(m_cur, axis=-1, keepdims=True))
                alpha = jnp.exp2((m_prev - m_new) * c)
                ps = [jnp.exp2((blk - m_new) * c) for blk in blocks]
                l_add = ps[0]
                for pb in ps[1:]:
                    l_add = l_add + pb
                l[rows, :] = alpha * l[rows, :] + l_add
                p = jnp.concatenate([pb.astype(_BF16) for pb in ps], axis=1)
                alpha2 = jnp.concatenate([alpha] * (D_V_DIM // LANES), axis=1)
                acc[rows, :] = alpha2 * acc[rows, :] + _dot(p, vv)
                m[rows, :] = m_new

    def body(j, carry):
        step(j, False)
        return carry

    lax.fori_loop(0, qi, body, 0)
    step(qi, True)

    lam = (jnp.exp(jnp.sum(lq1_ref[...] * lk1_ref[...], axis=-1, keepdims=True))
           - jnp.exp(jnp.sum(lq2_ref[...] * lk2_ref[...], axis=-1, keepdims=True)) + lam_init)
    o1 = acc1[...] / jnp.sum(l1[...], axis=-1, keepdims=True)
    o2 = acc2[...] / jnp.sum(l2[...], axis=-1, keepdims=True)
    hd = o1 - lam * o2
    y = _rmsnorm_rows(hd, nw_ref[...]) * (1.0 - lam_init)
    o_ref[...] = y.astype(o_ref.dtype)


def _attn(proj3, lq1, lk1, lq2, lk2, nw, lam_init, tile=512, sub=128):
    B, S, _ = proj3.shape
    w2 = 2 * D_QK_DIM
    lam_spec = pl.BlockSpec((1, D_QK_DIM), lambda b, h, i: (0, 0))
    return pl.pallas_call(
        functools.partial(_attn_kernel, lam_init=lam_init, tile=tile, sub=sub),
        grid=(B, D_HEADS, S // tile),
        in_specs=[
            lam_spec, lam_spec, lam_spec, lam_spec,
            pl.BlockSpec((None, tile, w2), lambda b, h, i: (b, i, OFF_DQ // w2 + h)),
            pl.BlockSpec((None, S, w2), lambda b, h, i: (b, 0, OFF_DK // w2 + h)),
            pl.BlockSpec((None, S, D_V_DIM), lambda b, h, i: (b, 0, OFF_DV // D_V_DIM + h)),
            pl.BlockSpec((1, D_V_DIM), lambda b, h, i: (0, h)),
        ],
        out_specs=pl.BlockSpec((None, tile, D_V_DIM), lambda b, h, i: (b, i, h)),
        out_shape=jax.ShapeDtypeStruct((B, S, D_V), _BF16),
        scratch_shapes=[
            pltpu.VMEM((tile, D_V_DIM), _F32),
            pltpu.VMEM((tile, D_V_DIM), _F32),
            pltpu.VMEM((tile, LANES), _F32),
            pltpu.VMEM((tile, LANES), _F32),
            pltpu.VMEM((tile, LANES), _F32),
            pltpu.VMEM((tile, LANES), _F32),
        ],
        compiler_params=_params(("parallel", "parallel", "arbitrary")),
        name="diffattn",
    )(lq1, lk1, lq2, lk2, proj3, proj3, proj3, nw)


def _merge_kernel(hm_ref, hd_ref, wm_ref, wd_ref, gm_ref, gd_ref, y_ref):
    a = _dot(hm_ref[...], wm_ref[...])
    b = _dot(hd_ref[...], wd_ref[...])
    y = _sigmoid(gm_ref[...].astype(_F32)) * a + _sigmoid(gd_ref[...].astype(_F32)) * b
    y_ref[...] = y.astype(y_ref.dtype)


def _merge(hm, hd, wm, wd, proj, tm=1024, tn=1024):
    T = hm.shape[0]
    return pl.pallas_call(
        _merge_kernel,
        grid=(T // tm, D_MODEL // tn),
        in_specs=[
            pl.BlockSpec((tm, M_V), lambda i, j: (i, 0)),
            pl.BlockSpec((tm, D_V), lambda i, j: (i, 0)),
            pl.BlockSpec((M_V, tn), lambda i, j: (0, j)),
            pl.BlockSpec((D_V, tn), lambda i, j: (0, j)),
            pl.BlockSpec((tm, tn), lambda i, j: (i, OFF_GM // tn + j)),
            pl.BlockSpec((tm, tn), lambda i, j: (i, OFF_GD // tn + j)),
        ],
        out_specs=pl.BlockSpec((tm, tn), lambda i, j: (i, j)),
        out_shape=jax.ShapeDtypeStruct((T, D_MODEL), _BF16),
        compiler_params=_params(("parallel", "parallel")),
        name="merge",
    )(hm, hd, wm, wd, proj, proj)


def _resmm_kernel(a_ref, w_ref, x_ref, o_ref):
    o_ref[...] = x_ref[...] + _dot(a_ref[...], w_ref[...])


def _resmm(a, w, x, tm, tn, name):
    T, K = a.shape
    N = w.shape[1]
    return pl.pallas_call(
        _resmm_kernel,
        grid=(T // tm, N // tn),
        in_specs=[
            pl.BlockSpec((tm, K), lambda i, j: (i, 0)),
            pl.BlockSpec((K, tn), lambda i, j: (0, j)),
            pl.BlockSpec((tm, tn), lambda i, j: (i, j)),
        ],
        out_specs=pl.BlockSpec((tm, tn), lambda i, j: (i, j)),
        out_shape=jax.ShapeDtypeStruct((T, N), _F32),
        compiler_params=_params(("parallel", "parallel")),
        name=name,
    )(a, w, x)


def _ffn_up_kernel(x_ref, nw_ref, wg_ref, wu_ref, a_ref, h_scr):
    @pl.when(pl.program_id(1) == 0)
    def _():
        h_scr[...] = _rmsnorm_rows(x_ref[...], nw_ref[...]).astype(_BF16)

    h = h_scr[...]
    g = _dot(h, wg_ref[...])
    u = _dot(h, wu_ref[...])
    a_ref[...] = (g * _sigmoid(g) * u).astype(a_ref.dtype)


def _ffn_up(x, nw, wg, wu, tm=1024, tn=512):
    T = x.shape[0]
    return pl.pallas_call(
        _ffn_up_kernel,
        grid=(T // tm, D_FF // tn),
        in_specs=[
            pl.BlockSpec((tm, D_MODEL), lambda i, j: (i, 0)),
            pl.BlockSpec((1, D_MODEL), lambda i, j: (0, 0)),
            pl.BlockSpec((D_MODEL, tn), lambda i, j: (0, j)),
            pl.BlockSpec((D_MODEL, tn), lambda i, j: (0, j)),
        ],
        out_specs=pl.BlockSpec((tm, tn), lambda i, j: (i, j)),
        out_shape=jax.ShapeDtypeStruct((T, D_FF), _BF16),
        scratch_shapes=[pltpu.VMEM((tm, D_MODEL), _BF16)],
        compiler_params=_params(("parallel", "arbitrary")),
        name="ffn_up",
    )(x, nw, wg, wu)


def _norm_kernel(x_ref, nw_ref, o_ref):
    o_ref[...] = _rmsnorm_rows(x_ref[...], nw_ref[...])


def _final_norm(x, nw, tm=512):
    T = x.shape[0]
    return pl.pallas_call(
        _norm_kernel,
        grid=(T // tm,),
        in_specs=[
            pl.BlockSpec((tm, D_MODEL), lambda i: (i, 0)),
            pl.BlockSpec((1, D_MODEL), lambda i: (0, 0)),
        ],
        out_specs=pl.BlockSpec((tm, D_MODEL), lambda i: (i, 0)),
        out_shape=jax.ShapeDtypeStruct((T, D_MODEL), _F32),
        compiler_params=_params(("parallel",)),
        name="final_norm",
    )(x, nw)


def kernel(x, attn_norm_w, w_in, conv_w, conv_b, b_igate, b_fgate, mlstm_norm_w, lambda_q1, lambda_k1, lambda_q2, lambda_k2, diff_norm_w, w_branch_m, w_branch_d, w_out, ffn_norm_w, w_ffn_gate, w_ffn_up, w_ffn_down, final_norm_w):
    B, S, D = x.shape
    T = B * S
    xf = x.reshape(T, D)
    gate_pad = jnp.zeros((LANES - 2 * M_HEADS,), _F32)
    for l in range(DEPTH):
        lam_init = 0.8 - 0.6 * math.exp(-0.3 * l)
        wl = w_in[l]
        w_proj = jnp.concatenate([wl[:, :IN_GATE_LO], wl[:, IN_GATE_HI:]], axis=1).astype(_BF16)
        w_gate = jnp.pad(wl[:, IN_GATE_LO:IN_GATE_HI], ((0, 0), (0, LANES - 2 * M_HEADS))).astype(_BF16)
        gate_bias = jnp.concatenate([b_igate[l], b_fgate[l], gate_pad]).reshape(1, LANES)

        proj, gpre = _inproj(xf, attn_norm_w[l].reshape(1, D), w_proj, w_gate)
        gcol, grow = _gates(gpre, gate_bias)
        proj3 = proj.reshape(B, S, N_PROJ)
        hm = _mlstm(proj3, conv_w[l], conv_b[l].reshape(1, 2 * M_QK), gcol.reshape(B, S, LANES), grow,
                    mlstm_norm_w[l].reshape(1, M_V))
        hd = _attn(proj3, lambda_q1[l].reshape(1, D_QK_DIM), lambda_k1[l].reshape(1, D_QK_DIM),
                   lambda_q2[l].reshape(1, D_QK_DIM), lambda_k2[l].reshape(1, D_QK_DIM),
                   diff_norm_w[l].reshape(1, D_V), lam_init)
        y = _merge(hm.reshape(T, M_V), hd.reshape(T, D_V), w_branch_m[l].astype(_BF16),
                   w_branch_d[l].astype(_BF16), proj)
        xf = _resmm(y, w_out[l].astype(_BF16), xf, 1024, 1024, "out_proj")
        a = _ffn_up(xf, ffn_norm_w[l].reshape(1, D), w_ffn_gate[l].astype(_BF16), w_ffn_up[l].astype(_BF16))
        xf = _resmm(a, w_ffn_down[l].astype(_BF16), xf, 1024, 512, "ffn_down")
    out = _final_norm(xf, final_norm_w.reshape(1, D))
    return out.reshape(B, S, D)
```

```python
import functools
import math

import jax
import jax.numpy as jnp
from jax import lax
from jax.experimental import pallas as pl
from jax.experimental.pallas import tpu as pltpu

D_MODEL = 2048
DEPTH = 4
M_HEADS = 8
M_QK_DIM = 128
M_V_DIM = 256
CONV_K = 4
M_QK = M_HEADS * M_QK_DIM
M_V = M_HEADS * M_V_DIM
D_HEADS = 8
D_QK_DIM = 128
D_V_DIM = 256
D_Q = D_HEADS * 2 * D_QK_DIM
D_V = D_HEADS * D_V_DIM
D_FF = 5632
EPS = 1e-6

IN_GATE_LO = 2 * M_QK + 2 * M_V
IN_GATE_HI = IN_GATE_LO + 2 * M_HEADS
OFF_MQ = 0
OFF_MK = M_QK
OFF_MV = 2 * M_QK
OFF_MO = OFF_MV + M_V
OFF_DQ = OFF_MO + M_V
OFF_DK = OFF_DQ + D_Q
OFF_DV = OFF_DK + D_Q
OFF_GM = OFF_DV + D_V
OFF_GD = OFF_GM + D_MODEL
N_PROJ = OFF_GD + D_MODEL

LANES = 128
SUBLANES = 8
M_CHUNK = 256
GATE_ROWS = 1024
VMEM_LIMIT = 56 * 1024 * 1024

_F32 = jnp.float32
_BF16 = jnp.bfloat16


def _dot(a, b):
    return jnp.dot(a, b, preferred_element_type=_F32)


def _dot_nt(a, b):
    return lax.dot_general(a, b, (((1,), (1,)), ((), ())), preferred_element_type=_F32)


def _dot_tn(a, b):
    return lax.dot_general(a, b, (((0,), (0,)), ((), ())), preferred_element_type=_F32)


def _sigmoid(x):
    return 1.0 / (1.0 + jnp.exp(-x))


def _rmsnorm_rows(x, w):
    ms = jnp.mean(x * x, axis=-1, keepdims=True)
    return x * lax.rsqrt(ms + EPS) * w


def _params(sem):
    return pltpu.CompilerParams(dimension_semantics=sem, vmem_limit_bytes=VMEM_LIMIT)


def _cast_kernel(w_ref, o_ref):
    o_ref[...] = w_ref[...].astype(o_ref.dtype)


def _cast_stacked(w, rows=256):
    L, K, N = w.shape
    return pl.pallas_call(
        _cast_kernel,
        grid=(L, K // rows),
        in_specs=[pl.BlockSpec((None, rows, N), lambda l, i: (l, i, 0))],
        out_specs=pl.BlockSpec((None, rows, N), lambda l, i: (l, i, 0)),
        out_shape=jax.ShapeDtypeStruct((L, K, N), _BF16),
        compiler_params=_params(("parallel", "parallel")),
        name="cast",
    )(w)


def _cast_win_kernel(w_ref, o_ref, g_ref):
    o_ref[:, :IN_GATE_LO] = w_ref[:, :IN_GATE_LO].astype(o_ref.dtype)
    o_ref[:, IN_GATE_LO:] = w_ref[:, IN_GATE_HI:].astype(o_ref.dtype)
    g = w_ref[:, IN_GATE_LO:IN_GATE_LO + LANES]
    lane = lax.broadcasted_iota(jnp.int32, g.shape, 1)
    g_ref[...] = jnp.where(lane < 2 * M_HEADS, g, 0.0).astype(g_ref.dtype)


def _cast_win(w_in, rows=128):
    L, K, N = w_in.shape
    return pl.pallas_call(
        _cast_win_kernel,
        grid=(L, K // rows),
        in_specs=[pl.BlockSpec((None, rows, N), lambda l, i: (l, i, 0))],
        out_specs=[
            pl.BlockSpec((None, rows, N_PROJ), lambda l, i: (l, i, 0)),
            pl.BlockSpec((None, rows, LANES), lambda l, i: (l, i, 0)),
        ],
        out_shape=[
            jax.ShapeDtypeStruct((L, K, N_PROJ), _BF16),
            jax.ShapeDtypeStruct((L, K, LANES), _BF16),
        ],
        compiler_params=_params(("parallel", "parallel")),
        name="cast_win",
    )(w_in)


def _inproj_kernel(x_ref, nw_ref, w_ref, wg_ref, o_ref, g_ref, h_scr):
    @pl.when(pl.program_id(1) == 0)
    def _():
        hb = _rmsnorm_rows(x_ref[...], nw_ref[...]).astype(_BF16)
        h_scr[...] = hb
        g_ref[...] = _dot(hb, wg_ref[...])

    o_ref[...] = _dot(h_scr[...], w_ref[...]).astype(o_ref.dtype)


def _inproj(x, nw, w, wg, layer, tm=1024, tn=1024):
    T = x.shape[0]
    return pl.pallas_call(
        _inproj_kernel,
        grid=(T // tm, N_PROJ // tn),
        in_specs=[
            pl.BlockSpec((tm, D_MODEL), lambda i, j: (i, 0)),
            pl.BlockSpec((1, D_MODEL), lambda i, j: (0, 0)),
            pl.BlockSpec((None, D_MODEL, tn), lambda i, j: (layer, 0, j)),
            pl.BlockSpec((None, D_MODEL, LANES), lambda i, j: (layer, 0, 0)),
        ],
        out_specs=[
            pl.BlockSpec((tm, tn), lambda i, j: (i, j)),
            pl.BlockSpec((tm, LANES), lambda i, j: (i, 0)),
        ],
        out_shape=[
            jax.ShapeDtypeStruct((T, N_PROJ), _BF16),
            jax.ShapeDtypeStruct((T, LANES), _F32),
        ],
        scratch_shapes=[pltpu.VMEM((tm, D_MODEL), _BF16)],
        compiler_params=_params(("parallel", "arbitrary")),
        name="inproj",
    )(x, nw, w, wg)


def _gates_kernel(g_ref, b_ref, gcol_ref, grow_ref):
    z = g_ref[...] + b_ref[...]
    rows = z.shape[0]
    lane = lax.broadcasted_iota(jnp.int32, z.shape, 1)
    lf = jnp.minimum(z, 0.0) - jnp.log1p(jnp.exp(-jnp.abs(z)))
    pos = lax.broadcasted_iota(jnp.int32, z.shape, 0) % M_CHUNK
    csum = lf
    d = 1
    while d < M_CHUNK:
        csum = csum + jnp.where(pos >= d, pltpu.roll(csum, d, 0), 0.0)
        d *= 2
    is_f = (lane >= M_HEADS) & (lane < 2 * M_HEADS)
    col = jnp.where(is_f, csum, z)
    gcol_ref[...] = col
    tr = col.T
    grow_ref[...] = tr[0:M_HEADS, :] - tr[M_HEADS:2 * M_HEADS, :]


def _gates(gpre, bias):
    T = gpre.shape[0]
    return pl.pallas_call(
        _gates_kernel,
        grid=(T // GATE_ROWS,),
        in_specs=[
            pl.BlockSpec((GATE_ROWS, LANES), lambda i: (i, 0)),
            pl.BlockSpec((1, LANES), lambda i: (0, 0)),
        ],
        out_specs=[
            pl.BlockSpec((GATE_ROWS, LANES), lambda i: (i, 0)),
            pl.BlockSpec((M_HEADS, GATE_ROWS), lambda i: (0, i)),
        ],
        out_shape=[
            jax.ShapeDtypeStruct((T, LANES), _F32),
            jax.ShapeDtypeStruct((M_HEADS, T), _F32),
        ],
        compiler_params=_params(("parallel",)),
        name="gates",
    )(gpre, bias)


def _mlstm_kernel(q_ref, k_ref, v_ref, o_ref, cwq_ref, cbq_ref, cwk_ref, cbk_ref,
                  gcol_ref, grow_ref, nw_ref, out_ref, xq_scr, xk_scr, c_scr, m_scr):
    head = pl.program_id(1)
    S = q_ref.shape[0]
    L = M_CHUNK
    pad = SUBLANES
    zpad = jnp.zeros((pad, M_QK_DIM), _F32)
    xq_scr[0:pad, :] = zpad
    xk_scr[0:pad, :] = zpad
    xq_scr[pad:, :] = q_ref[...].astype(_F32)
    xk_scr[pad:, :] = k_ref[...].astype(_F32)
    c_scr[...] = jnp.zeros(c_scr.shape, _F32)
    m_scr[...] = jnp.zeros(m_scr.shape, _F32)

    lane = lax.broadcasted_iota(jnp.int32, (L, LANES), 1)
    row_i = lax.broadcasted_iota(jnp.int32, (L, L), 0)
    col_i = lax.broadcasted_iota(jnp.int32, (L, L), 1)
    tril = col_i <= row_i
    ones_aug = jnp.ones((L, LANES), _BF16)
    k_scale = M_QK_DIM ** -0.5

    def conv_silu(x_scr, cw_ref, cb_ref, r0):
        win = x_scr[pl.ds(r0, L + pad), :]
        acc = cb_ref[...] + cw_ref[CONV_K - 1:CONV_K, :] * win[pad:, :]
        for j in range(CONV_K - 1):
            acc = acc + cw_ref[j:j + 1, :] * pltpu.roll(win, CONV_K - 1 - j, 0)[pad:, :]
        return acc * _sigmoid(acc)

    def chunk(c, carry):
        r0 = pl.multiple_of(c * L, L)
        qf = conv_silu(xq_scr, cwq_ref, cbq_ref, r0)
        kf = conv_silu(xk_scr, cwk_ref, cbk_ref, r0) * k_scale
        qb = qf.astype(_BF16)
        kb = kf.astype(_BF16)
        vaug = jnp.concatenate([v_ref[pl.ds(r0, L), :], ones_aug], axis=1)

        g = gcol_ref[pl.ds(r0, L), :]
        ig = jnp.sum(jnp.where(lane == head, g, 0.0), axis=-1, keepdims=True)
        b = jnp.sum(jnp.where(lane == head + M_HEADS, g, 0.0), axis=-1, keepdims=True)
        r = grow_ref[pl.ds(head, 1), pl.ds(r0, L)]
        m_prev = m_scr[...]

        log_d = jnp.where(tril, b + r, -jnp.inf)
        log_inter = b + m_prev
        m_row = jnp.maximum(log_inter, jnp.max(log_d, axis=-1, keepdims=True))
        dmat = jnp.exp(log_d - m_row)
        inter = jnp.exp(log_inter - m_row)
        s = _dot_nt(qb, kb) * dmat
        num_aug = _dot(s.astype(_BF16), vaug) + inter * _dot(qb, c_scr[...].astype(_BF16))
        num = num_aug[:, :M_V_DIM]
        den = num_aug[:, M_V_DIM:M_V_DIM + 1]
        hh = num / jnp.maximum(jnp.abs(den), jnp.exp(-m_row))
        y = _rmsnorm_rows(hh, nw_ref[...])
        gate = _sigmoid(o_ref[pl.ds(r0, L), :].astype(_F32))
        out_ref[pl.ds(r0, L), :] = (gate * y).astype(out_ref.dtype)

        b_last = b[L - 1:L, :]
        log_w = b_last - b + ig
        m_new = jnp.maximum(b_last + m_prev, jnp.max(log_w, axis=0, keepdims=True))
        w = jnp.exp(log_w - m_new)
        decay = jnp.exp(b_last + m_prev - m_new)
        kw = (kf * w).astype(_BF16)
        c_scr[...] = decay * c_scr[...] + _dot_tn(kw, vaug)
        m_scr[...] = m_new
        return carry

    lax.fori_loop(0, S // L, chunk, 0)


def _mlstm(proj3, conv_w, conv_b, gcol3, grow, nw):
    B, S, _ = proj3.shape
    qk_blk = lambda off: (lambda b, h: (b, 0, off // M_QK_DIM + h))
    v_blk = lambda off: (lambda b, h: (b, 0, off // M_V_DIM + h))
    return pl.pallas_call(
        _mlstm_kernel,
        grid=(B, M_HEADS),
        in_specs=[
            pl.BlockSpec((None, S, M_QK_DIM), qk_blk(OFF_MQ)),
            pl.BlockSpec((None, S, M_QK_DIM), qk_blk(OFF_MK)),
            pl.BlockSpec((None, S, M_V_DIM), v_blk(OFF_MV)),
            pl.BlockSpec((None, S, M_V_DIM), v_blk(OFF_MO)),
            pl.BlockSpec((CONV_K, M_QK_DIM), lambda b, h: (0, h)),
            pl.BlockSpec((1, M_QK_DIM), lambda b, h: (0, h)),
            pl.BlockSpec((CONV_K, M_QK_DIM), lambda b, h: (0, M_HEADS + h)),
            pl.BlockSpec((1, M_QK_DIM), lambda b, h: (0, M_HEADS + h)),
            pl.BlockSpec((None, S, LANES), lambda b, h: (b, 0, 0)),
            pl.BlockSpec((M_HEADS, S), lambda b, h: (0, b)),
            pl.BlockSpec((1, M_V_DIM), lambda b, h: (0, h)),
        ],
        out_specs=pl.BlockSpec((None, S, M_V_DIM), lambda b, h: (b, 0, h)),
        out_shape=jax.ShapeDtypeStruct((B, S, M_V), _BF16),
        scratch_shapes=[
            pltpu.VMEM((S + SUBLANES, M_QK_DIM), _F32),
            pltpu.VMEM((S + SUBLANES, M_QK_DIM), _F32),
            pltpu.VMEM((M_QK_DIM, M_V_DIM + LANES), _F32),
            pltpu.VMEM((1, 1), _F32),
        ],
        compiler_params=_params(("parallel", "parallel")),
        name="mlstm",
    )(proj3, proj3, proj3, proj3, conv_w, conv_b, conv_w, conv_b, gcol3, grow, nw)


def _attn_kernel(lq1_ref, lk1_ref, lq2_ref, lk2_ref, q_ref, k_ref, v_ref, nw_ref, o_ref,
                 qs_scr, acc1, acc2, m1, l1, m2, l2, *, lam_init, tq, tk, sub):
    qi = pl.program_id(2)
    d = D_QK_DIM
    qs_scr[...] = (q_ref[...].astype(_F32) * ((d ** -0.5) * math.log2(math.e))).astype(_BF16)
    n_lane_blk = tk // LANES
    kv_per_q = tq // tk
    streams = ((acc1, m1, l1), (acc2, m2, l2))
    for acc, m, l in streams:
        acc[...] = jnp.zeros(acc.shape, _F32)
        m[...] = jnp.full(m.shape, -jnp.inf, _F32)
        l[...] = jnp.zeros(l.shape, _F32)

    row_i = lax.broadcasted_iota(jnp.int32, (sub, LANES), 0)
    col_i = lax.broadcasted_iota(jnp.int32, (sub, LANES), 1)

    def step(j, diag):
        r0 = pl.multiple_of(j * tk, tk)
        for idx, (acc, m, l) in enumerate(streams):
            kk = k_ref[pl.ds(r0, tk), idx * d:(idx + 1) * d]
            vv = v_ref[pl.ds(r0, tk), :]
            for r in range(tq // sub):
                if diag is not None and (r + 1) * sub - 1 < diag * tk:
                    continue
                rows = pl.ds(r * sub, sub)
                s = _dot_nt(qs_scr[rows, idx * d:(idx + 1) * d], kk)
                blocks = [s[:, b * LANES:(b + 1) * LANES] for b in range(n_lane_blk)]
                if diag is not None:
                    blocks = [blk if diag * tk + (b + 1) * LANES - 1 <= r * sub else
                              jnp.where(col_i + (diag * tk + b * LANES) <= row_i + r * sub, blk, -jnp.inf)
                              for b, blk in enumerate(blocks)]
                m_cur = blocks[0]
                for blk in blocks[1:]:
                    m_cur = jnp.maximum(m_cur, blk)
                m_prev = m[rows, :]
                m_new = jnp.maximum(m_prev, jnp.max(m_cur, axis=-1, keepdims=True))
                alpha = jnp.exp2(m_prev - m_new)
                ps = [jnp.exp2(blk - m_new) for blk in blocks]
                l_add = ps[0]
                for pb in ps[1:]:
                    l_add = l_add + pb
                l[rows, :] = alpha * l[rows, :] + l_add
                p = jnp.concatenate([pb.astype(_BF16) for pb in ps], axis=1)
                alpha2 = jnp.concatenate([alpha] * (D_V_DIM // LANES), axis=1)
                acc[rows, :] = alpha2 * acc[rows, :] + _dot(p, vv)
                m[rows, :] = m_new

    def body(j, carry):
        step(j, None)
        return carry

    lax.fori_loop(0, qi * kv_per_q, body, 0)
    for t in range(kv_per_q):
        step(qi * kv_per_q + t, t)

    lam = (jnp.exp(jnp.sum(lq1_ref[...] * lk1_ref[...], axis=-1, keepdims=True))
           - jnp.exp(jnp.sum(lq2_ref[...] * lk2_ref[...], axis=-1, keepdims=True)) + lam_init)
    o1 = acc1[...] / jnp.sum(l1[...], axis=-1, keepdims=True)
    o2 = acc2[...] / jnp.sum(l2[...], axis=-1, keepdims=True)
    hd = o1 - lam * o2
    y = _rmsnorm_rows(hd, nw_ref[...]) * (1.0 - lam_init)
    o_ref[...] = y.astype(o_ref.dtype)


def _attn(proj3, lq1, lk1, lq2, lk2, nw, lam_init, tq=2048, tk=512, sub=128):
    B, S, _ = proj3.shape
    w2 = 2 * D_QK_DIM
    lam_spec = pl.BlockSpec((1, D_QK_DIM), lambda b, h, i: (0, 0))
    return pl.pallas_call(
        functools.partial(_attn_kernel, lam_init=lam_init, tq=tq, tk=tk, sub=sub),
        grid=(B, D_HEADS, S // tq),
        in_specs=[
            lam_spec, lam_spec, lam_spec, lam_spec,
            pl.BlockSpec((None, tq, w2), lambda b, h, i: (b, i, OFF_DQ // w2 + h)),
            pl.BlockSpec((None, S, w2), lambda b, h, i: (b, 0, OFF_DK // w2 + h)),
            pl.BlockSpec((None, S, D_V_DIM), lambda b, h, i: (b, 0, OFF_DV // D_V_DIM + h)),
            pl.BlockSpec((1, D_V_DIM), lambda b, h, i: (0, h)),
        ],
        out_specs=pl.BlockSpec((None, tq, D_V_DIM), lambda b, h, i: (b, i, h)),
        out_shape=jax.ShapeDtypeStruct((B, S, D_V), _BF16),
        scratch_shapes=[
            pltpu.VMEM((tq, w2), _BF16),
            pltpu.VMEM((tq, D_V_DIM), _F32),
            pltpu.VMEM((tq, D_V_DIM), _F32),
            pltpu.VMEM((tq, LANES), _F32),
            pltpu.VMEM((tq, LANES), _F32),
            pltpu.VMEM((tq, LANES), _F32),
            pltpu.VMEM((tq, LANES), _F32),
        ],
        compiler_params=_params(("parallel", "parallel", "arbitrary")),
        name="diffattn",
    )(lq1, lk1, lq2, lk2, proj3, proj3, proj3, nw)


def _merge_kernel(hm_ref, hd_ref, wm_ref, wd_ref, gm_ref, gd_ref, y_ref):
    a = _dot(hm_ref[...], wm_ref[...])
    b = _dot(hd_ref[...], wd_ref[...])
    y = _sigmoid(gm_ref[...].astype(_F32)) * a + _sigmoid(gd_ref[...].astype(_F32)) * b
    y_ref[...] = y.astype(y_ref.dtype)


def _merge(hm, hd, wm, wd, proj, layer, tm=1024, tn=1024):
    T = hm.shape[0]
    return pl.pallas_call(
        _merge_kernel,
        grid=(T // tm, D_MODEL // tn),
        in_specs=[
            pl.BlockSpec((tm, M_V), lambda i, j: (i, 0)),
            pl.BlockSpec((tm, D_V), lambda i, j: (i, 0)),
            pl.BlockSpec((None, M_V, tn), lambda i, j: (layer, 0, j)),
            pl.BlockSpec((None, D_V, tn), lambda i, j: (layer, 0, j)),
            pl.BlockSpec((tm, tn), lambda i, j: (i, OFF_GM // tn + j)),
            pl.BlockSpec((tm, tn), lambda i, j: (i, OFF_GD // tn + j)),
        ],
        out_specs=pl.BlockSpec((tm, tn), lambda i, j: (i, j)),
        out_shape=jax.ShapeDtypeStruct((T, D_MODEL), _BF16),
        compiler_params=_params(("parallel", "parallel")),
        name="merge",
    )(hm, hd, wm, wd, proj, proj)


def _resmm_kernel(a_ref, w_ref, x_ref, o_ref):
    o_ref[...] = x_ref[...] + _dot(a_ref[...], w_ref[...])


def _resmm(a, w, x, layer, tm, tn, name):
    T, K = a.shape
    N = w.shape[2]
    return pl.pallas_call(
        _resmm_kernel,
        grid=(T // tm, N // tn),
        in_specs=[
            pl.BlockSpec((tm, K), lambda i, j: (i, 0)),
            pl.BlockSpec((None, K, tn), lambda i, j: (layer, 0, j)),
            pl.BlockSpec((tm, tn), lambda i, j: (i, j)),
        ],
        out_specs=pl.BlockSpec((tm, tn), lambda i, j: (i, j)),
        out_shape=jax.ShapeDtypeStruct((T, N), _F32),
        compiler_params=_params(("parallel", "parallel")),
        name=name,
    )(a, w, x)


def _ffn_up_kernel(x_ref, nw_ref, wg_ref, wu_ref, a_ref, h_scr):
    @pl.when(pl.program_id(1) == 0)
    def _():
        h_scr[...] = _rmsnorm_rows(x_ref[...], nw_ref[...]).astype(_BF16)

    h = h_scr[...]
    g = _dot(h, wg_ref[...])
    u = _dot(h, wu_ref[...])
    a_ref[...] = (g * _sigmoid(g) * u).astype(a_ref.dtype)


def _ffn_up(x, nw, wg, wu, layer, tm=1024, tn=512):
    T = x.shape[0]
    return pl.pallas_call(
        _ffn_up_kernel,
        grid=(T // tm, D_FF // tn),
        in_specs=[
            pl.BlockSpec((tm, D_MODEL), lambda i, j: (i, 0)),
            pl.BlockSpec((1, D_MODEL), lambda i, j: (0, 0)),
            pl.BlockSpec((None, D_MODEL, tn), lambda i, j: (layer, 0, j)),
            pl.BlockSpec((None, D_MODEL, tn), lambda i, j: (layer, 0, j)),
        ],
        out_specs=pl.BlockSpec((tm, tn), lambda i, j: (i, j)),
        out_shape=jax.ShapeDtypeStruct((T, D_FF), _BF16),
        scratch_shapes=[pltpu.VMEM((tm, D_MODEL), _BF16)],
        compiler_params=_params(("parallel", "arbitrary")),
        name="ffn_up",
    )(x, nw, wg, wu)


def _norm_kernel(x_ref, nw_ref, o_ref):
    o_ref[...] = _rmsnorm_rows(x_ref[...], nw_ref[...])


def _final_norm(x, nw, tm=512):
    T = x.shape[0]
    return pl.pallas_call(
        _norm_kernel,
        grid=(T // tm,),
        in_specs=[
            pl.BlockSpec((tm, D_MODEL), lambda i: (i, 0)),
            pl.BlockSpec((1, D_MODEL), lambda i: (0, 0)),
        ],
        out_specs=pl.BlockSpec((tm, D_MODEL), lambda i: (i, 0)),
        out_shape=jax.ShapeDtypeStruct((T, D_MODEL), _F32),
        compiler_params=_params(("parallel",)),
        name="final_norm",
    )(x, nw)


def kernel(x, attn_norm_w, w_in, conv_w, conv_b, b_igate, b_fgate, mlstm_norm_w, lambda_q1, lambda_k1, lambda_q2, lambda_k2, diff_norm_w, w_branch_m, w_branch_d, w_out, ffn_norm_w, w_ffn_gate, w_ffn_up, w_ffn_down, final_norm_w):
    B, S, D = x.shape
    T = B * S
    xf = x.reshape(T, D)
    gate_pad = jnp.zeros((LANES - 2 * M_HEADS,), _F32)
    w_proj, w_gate = _cast_win(w_in)
    w_bm = _cast_stacked(w_branch_m)
    w_bd = _cast_stacked(w_branch_d)
    w_o = _cast_stacked(w_out)
    w_fg = _cast_stacked(w_ffn_gate)
    w_fu = _cast_stacked(w_ffn_up)
    w_fd = _cast_stacked(w_ffn_down)
    for l in range(DEPTH):
        lam_init = 0.8 - 0.6 * math.exp(-0.3 * l)
        gate_bias = jnp.concatenate([b_igate[l], b_fgate[l], gate_pad]).reshape(1, LANES)

        proj, gpre = _inproj(xf, attn_norm_w[l].reshape(1, D), w_proj, w_gate, l)
        gcol, grow = _gates(gpre, gate_bias)
        proj3 = proj.reshape(B, S, N_PROJ)
        hm = _mlstm(proj3, conv_w[l], conv_b[l].reshape(1, 2 * M_QK), gcol.reshape(B, S, LANES), grow,
                    mlstm_norm_w[l].reshape(1, M_V))
        hd = _attn(proj3, lambda_q1[l].reshape(1, D_QK_DIM), lambda_k1[l].reshape(1, D_QK_DIM),
                   lambda_q2[l].reshape(1, D_QK_DIM), lambda_k2[l].reshape(1, D_QK_DIM),
                   diff_norm_w[l].reshape(1, D_V), lam_init)
        y = _merge(hm.reshape(T, M_V), hd.reshape(T, D_V), w_bm, w_bd, proj, l)
        xf = _resmm(y, w_o, xf, l, 1024, 1024, "out_proj")
        a = _ffn_up(xf, ffn_norm_w[l].reshape(1, D), w_fg, w_fu, l)
        xf = _resmm(a, w_fd, xf, l, 1024, 512, "ffn_down")
    out = _final_norm(xf, final_norm_w.reshape(1, D))
    return out.reshape(B, S, D)
```

```python
import functools
import math

import jax
import jax.numpy as jnp
from jax import lax
from jax.experimental import pallas as pl
from jax.experimental.pallas import tpu as pltpu

D_MODEL = 2048
DEPTH = 4
M_HEADS = 8
M_QK_DIM = 128
M_V_DIM = 256
CONV_K = 4
M_QK = M_HEADS * M_QK_DIM
M_V = M_HEADS * M_V_DIM
D_HEADS = 8
D_QK_DIM = 128
D_V_DIM = 256
D_Q = D_HEADS * 2 * D_QK_DIM
D_V = D_HEADS * D_V_DIM
D_FF = 5632
EPS = 1e-6

IN_GATE_LO = 2 * M_QK + 2 * M_V
IN_GATE_HI = IN_GATE_LO + 2 * M_HEADS
OFF_MQ = 0
OFF_MK = M_QK
OFF_MV = 2 * M_QK
OFF_MO = OFF_MV + M_V
OFF_DQ = OFF_MO + M_V
OFF_DK = OFF_DQ + D_Q
OFF_DV = OFF_DK + D_Q
OFF_GM = OFF_DV + D_V
OFF_GD = OFF_GM + D_MODEL
N_PROJ = OFF_GD + D_MODEL

LANES = 128
SUBLANES = 8
M_CHUNK = 256
GATE_ROWS = 1024
VMEM_LIMIT = 56 * 1024 * 1024

_F32 = jnp.float32
_BF16 = jnp.bfloat16


def _dot(a, b):
    return jnp.dot(a, b, preferred_element_type=_F32)


def _dot_nt(a, b):
    return lax.dot_general(a, b, (((1,), (1,)), ((), ())), preferred_element_type=_F32)


def _dot_tn(a, b):
    return lax.dot_general(a, b, (((0,), (0,)), ((), ())), preferred_element_type=_F32)


def _sigmoid(x):
    return 1.0 / (1.0 + jnp.exp(-x))


def _rmsnorm_rows(x, w):
    ms = jnp.mean(x * x, axis=-1, keepdims=True)
    return x * lax.rsqrt(ms + EPS) * w


def _params(sem):
    return pltpu.CompilerParams(dimension_semantics=sem, vmem_limit_bytes=VMEM_LIMIT)


def _cast_kernel(w_ref, o_ref):
    o_ref[...] = w_ref[...].astype(o_ref.dtype)


def _cast_stacked(w, rows=256):
    L, K, N = w.shape
    return pl.pallas_call(
        _cast_kernel,
        grid=(L, K // rows),
        in_specs=[pl.BlockSpec((None, rows, N), lambda l, i: (l, i, 0))],
        out_specs=pl.BlockSpec((None, rows, N), lambda l, i: (l, i, 0)),
        out_shape=jax.ShapeDtypeStruct((L, K, N), _BF16),
        compiler_params=_params(("parallel", "parallel")),
        name="cast",
    )(w)


def _cast_win_t(w_in_t, rows=512):
    L, _, K = w_in_t.shape
    n_lo = IN_GATE_LO // rows

    def src_row(l, i):
        start = jnp.where(i < n_lo, i * rows, IN_GATE_HI + (i - n_lo) * rows)
        return (l, pl.multiple_of(start, 2 * M_HEADS), 0)

    def cast_rows(w_ref, o_ref):
        o_ref[...] = w_ref[0].astype(o_ref.dtype)

    return pl.pallas_call(
        cast_rows,
        grid=(L, N_PROJ // rows),
        in_specs=[pl.BlockSpec((pl.Element(1), pl.Element(rows), pl.Element(K)), src_row)],
        out_specs=pl.BlockSpec((None, rows, K), lambda l, i: (l, i, 0)),
        out_shape=jax.ShapeDtypeStruct((L, N_PROJ, K), _BF16),
        compiler_params=_params(("parallel", "parallel")),
        name="cast_win",
    )(w_in_t)


def _inproj_kernel(x_ref, nw_ref, w_ref, wg_ref, o_ref, g_ref, h_scr):
    @pl.when(pl.program_id(1) == 0)
    def _():
        hb = _rmsnorm_rows(x_ref[...], nw_ref[...]).astype(_BF16)
        h_scr[...] = hb
        g_ref[...] = _dot_nt(hb, wg_ref[...])

    o_ref[...] = _dot_nt(h_scr[...], w_ref[...]).astype(o_ref.dtype)


def _inproj(x, nw, w, wg, layer, tm=1024, tn=1024):
    T = x.shape[0]
    return pl.pallas_call(
        _inproj_kernel,
        grid=(T // tm, N_PROJ // tn),
        in_specs=[
            pl.BlockSpec((tm, D_MODEL), lambda i, j: (i, 0)),
            pl.BlockSpec((1, D_MODEL), lambda i, j: (0, 0)),
            pl.BlockSpec((None, tn, D_MODEL), lambda i, j: (layer, j, 0)),
            pl.BlockSpec((None, LANES, D_MODEL), lambda i, j: (layer, 0, 0)),
        ],
        out_specs=[
            pl.BlockSpec((tm, tn), lambda i, j: (i, j)),
            pl.BlockSpec((tm, LANES), lambda i, j: (i, 0)),
        ],
        out_shape=[
            jax.ShapeDtypeStruct((T, N_PROJ), _BF16),
            jax.ShapeDtypeStruct((T, LANES), _F32),
        ],
        scratch_shapes=[pltpu.VMEM((tm, D_MODEL), _BF16)],
        compiler_params=_params(("parallel", "arbitrary")),
        name="inproj",
    )(x, nw, w, wg)


def _gates_kernel(g_ref, b_ref, gcol_ref, grow_ref):
    z = g_ref[...] + b_ref[...]
    rows = z.shape[0]
    lane = lax.broadcasted_iota(jnp.int32, z.shape, 1)
    lf = jnp.minimum(z, 0.0) - jnp.log1p(jnp.exp(-jnp.abs(z)))
    pos = lax.broadcasted_iota(jnp.int32, z.shape, 0) % M_CHUNK
    csum = lf
    d = 1
    while d < M_CHUNK:
        csum = csum + jnp.where(pos >= d, pltpu.roll(csum, d, 0), 0.0)
        d *= 2
    is_f = (lane >= M_HEADS) & (lane < 2 * M_HEADS)
    col = jnp.where(is_f, csum, z)
    gcol_ref[...] = col
    tr = col.T
    grow_ref[...] = tr[0:M_HEADS, :] - tr[M_HEADS:2 * M_HEADS, :]


def _gates(gpre, bias):
    T = gpre.shape[0]
    return pl.pallas_call(
        _gates_kernel,
        grid=(T // GATE_ROWS,),
        in_specs=[
            pl.BlockSpec((GATE_ROWS, LANES), lambda i: (i, 0)),
            pl.BlockSpec((1, LANES), lambda i: (0, 0)),
        ],
        out_specs=[
            pl.BlockSpec((GATE_ROWS, LANES), lambda i: (i, 0)),
            pl.BlockSpec((M_HEADS, GATE_ROWS), lambda i: (0, i)),
        ],
        out_shape=[
            jax.ShapeDtypeStruct((T, LANES), _F32),
            jax.ShapeDtypeStruct((M_HEADS, T), _F32),
        ],
        compiler_params=_params(("parallel",)),
        name="gates",
    )(gpre, bias)


def _mlstm_kernel(q_ref, k_ref, v_ref, o_ref, cwq_ref, cbq_ref, cwk_ref, cbk_ref,
                  gcol_ref, grow_ref, nw_ref, out_ref, xq_scr, xk_scr, c_scr, m_scr):
    head = pl.program_id(1)
    S = q_ref.shape[0]
    L = M_CHUNK
    pad = SUBLANES
    zpad = jnp.zeros((pad, M_QK_DIM), _F32)
    xq_scr[0:pad, :] = zpad
    xk_scr[0:pad, :] = zpad
    xq_scr[pad:, :] = q_ref[...].astype(_F32)
    xk_scr[pad:, :] = k_ref[...].astype(_F32)
    c_scr[...] = jnp.zeros(c_scr.shape, _F32)
    m_scr[...] = jnp.zeros(m_scr.shape, _F32)

    lane = lax.broadcasted_iota(jnp.int32, (L, LANES), 1)
    row_i = lax.broadcasted_iota(jnp.int32, (L, L), 0)
    col_i = lax.broadcasted_iota(jnp.int32, (L, L), 1)
    tril = col_i <= row_i
    ones_aug = jnp.ones((L, LANES), _BF16)
    k_scale = M_QK_DIM ** -0.5

    def conv_silu(x_scr, cw_ref, cb_ref, r0):
        win = x_scr[pl.ds(r0, L + pad), :]
        acc = cb_ref[...] + cw_ref[CONV_K - 1:CONV_K, :] * win[pad:, :]
        for j in range(CONV_K - 1):
            acc = acc + cw_ref[j:j + 1, :] * pltpu.roll(win, CONV_K - 1 - j, 0)[pad:, :]
        return acc * _sigmoid(acc)

    def chunk(c, carry):
        r0 = pl.multiple_of(c * L, L)
        qf = conv_silu(xq_scr, cwq_ref, cbq_ref, r0)
        kf = conv_silu(xk_scr, cwk_ref, cbk_ref, r0) * k_scale
        qb = qf.astype(_BF16)
        kb = kf.astype(_BF16)
        vaug = jnp.concatenate([v_ref[pl.ds(r0, L), :], ones_aug], axis=1)

        g = gcol_ref[pl.ds(r0, L), :]
        ig = jnp.sum(jnp.where(lane == head, g, 0.0), axis=-1, keepdims=True)
        b = jnp.sum(jnp.where(lane == head + M_HEADS, g, 0.0), axis=-1, keepdims=True)
        r = grow_ref[pl.ds(head, 1), pl.ds(r0, L)]
        m_prev = m_scr[...]

        log_d = jnp.where(tril, b + r, -jnp.inf)
        log_inter = b + m_prev
        m_row = jnp.maximum(log_inter, jnp.max(log_d, axis=-1, keepdims=True))
        dmat = jnp.exp(log_d - m_row)
        inter = jnp.exp(log_inter - m_row)
        s = _dot_nt(qb, kb) * dmat
        num_aug = _dot(s.astype(_BF16), vaug) + inter * _dot(qb, c_scr[...].astype(_BF16))
        num = num_aug[:, :M_V_DIM]
        den = num_aug[:, M_V_DIM:M_V_DIM + 1]
        hh = num / jnp.maximum(jnp.abs(den), jnp.exp(-m_row))
        y = _rmsnorm_rows(hh, nw_ref[...])
        gate = _sigmoid(o_ref[pl.ds(r0, L), :].astype(_F32))
        out_ref[pl.ds(r0, L), :] = (gate * y).astype(out_ref.dtype)

        b_last = b[L - 1:L, :]
        log_w = b_last - b + ig
        m_new = jnp.maximum(b_last + m_prev, jnp.max(log_w, axis=0, keepdims=True))
        w = jnp.exp(log_w - m_new)
        decay = jnp.exp(b_last + m_prev - m_new)
        kw = (kf * w).astype(_BF16)
        c_scr[...] = decay * c_scr[...] + _dot_tn(kw, vaug)
        m_scr[...] = m_new
        return carry

    lax.fori_loop(0, S // L, chunk, 0)


def _mlstm(proj3, conv_w, conv_b, gcol3, grow, nw):
    B, S, _ = proj3.shape
    qk_blk = lambda off: (lambda b, h: (b, 0, off // M_QK_DIM + h))
    v_blk = lambda off: (lambda b, h: (b, 0, off // M_V_DIM + h))
    return pl.pallas_call(
        _mlstm_kernel,
        grid=(B, M_HEADS),
        in_specs=[
            pl.BlockSpec((None, S, M_QK_DIM), qk_blk(OFF_MQ)),
            pl.BlockSpec((None, S, M_QK_DIM), qk_blk(OFF_MK)),
            pl.BlockSpec((None, S, M_V_DIM), v_blk(OFF_MV)),
            pl.BlockSpec((None, S, M_V_DIM), v_blk(OFF_MO)),
            pl.BlockSpec((CONV_K, M_QK_DIM), lambda b, h: (0, h)),
            pl.BlockSpec((1, M_QK_DIM), lambda b, h: (0, h)),
            pl.BlockSpec((CONV_K, M_QK_DIM), lambda b, h: (0, M_HEADS + h)),
            pl.BlockSpec((1, M_QK_DIM), lambda b, h: (0, M_HEADS + h)),
            pl.BlockSpec((None, S, LANES), lambda b, h: (b, 0, 0)),
            pl.BlockSpec((M_HEADS, S), lambda b, h: (0, b)),
            pl.BlockSpec((1, M_V_DIM), lambda b, h: (0, h)),
        ],
        out_specs=pl.BlockSpec((None, S, M_V_DIM), lambda b, h: (b, 0, h)),
        out_shape=jax.ShapeDtypeStruct((B, S, M_V), _BF16),
        scratch_shapes=[
            pltpu.VMEM((S + SUBLANES, M_QK_DIM), _F32),
            pltpu.VMEM((S + SUBLANES, M_QK_DIM), _F32),
            pltpu.VMEM((M_QK_DIM, M_V_DIM + LANES), _F32),
            pltpu.VMEM((1, 1), _F32),
        ],
        compiler_params=_params(("parallel", "parallel")),
        name="mlstm",
    )(proj3, proj3, proj3, proj3, conv_w, conv_b, conv_w, conv_b, gcol3, grow, nw)


def _attn_kernel(lq1_ref, lk1_ref, lq2_ref, lk2_ref, q_ref, k_ref, v_ref, nw_ref, o_ref,
                 qs_scr, acc1, acc2, m1, l1, m2, l2, *, lam_init, tq, tk, sub):
    qi = pl.program_id(2)
    d = D_QK_DIM
    qs_scr[...] = (q_ref[...].astype(_F32) * ((d ** -0.5) * math.log2(math.e))).astype(_BF16)
    n_lane_blk = tk // LANES
    kv_per_q = tq // tk
    streams = ((acc1, m1, l1), (acc2, m2, l2))
    for acc, m, l in streams:
        acc[...] = jnp.zeros(acc.shape, _F32)
        m[...] = jnp.full(m.shape, -jnp.inf, _F32)
        l[...] = jnp.zeros(l.shape, _F32)

    row_i = lax.broadcasted_iota(jnp.int32, (sub, LANES), 0)
    col_i = lax.broadcasted_iota(jnp.int32, (sub, LANES), 1)

    def step(j, diag):
        r0 = pl.multiple_of(j * tk, tk)
        for idx, (acc, m, l) in enumerate(streams):
            kk = k_ref[pl.ds(r0, tk), idx * d:(idx + 1) * d]
            vv = v_ref[pl.ds(r0, tk), :]
            for r in range(tq // sub):
                if diag is not None and (r + 1) * sub - 1 < diag * tk:
                    continue
                rows = pl.ds(r * sub, sub)
                s = _dot_nt(qs_scr[rows, idx * d:(idx + 1) * d], kk)
                blocks = [s[:, b * LANES:(b + 1) * LANES] for b in range(n_lane_blk)]
                if diag is not None:
                    blocks = [blk if diag * tk + (b + 1) * LANES - 1 <= r * sub else
                              jnp.where(col_i + (diag * tk + b * LANES) <= row_i + r * sub, blk, -jnp.inf)
                              for b, blk in enumerate(blocks)]
                m_cur = blocks[0]
                for blk in blocks[1:]:
                    m_cur = jnp.maximum(m_cur, blk)
                m_prev = m[rows, :]
                m_new = jnp.maximum(m_prev, jnp.max(m_cur, axis=-1, keepdims=True))
                alpha = jnp.exp2(m_prev - m_new)
                ps = [jnp.exp2(blk - m_new) for blk in blocks]
                l_add = ps[0]
                for pb in ps[1:]:
                    l_add = l_add + pb
                l[rows, :] = alpha * l[rows, :] + l_add
                p = jnp.concatenate([pb.astype(_BF16) for pb in ps], axis=1)
                alpha2 = jnp.concatenate([alpha] * (D_V_DIM // LANES), axis=1)
                acc[rows, :] = alpha2 * acc[rows, :] + _dot(p, vv)
                m[rows, :] = m_new

    def body(j, carry):
        step(j, None)
        return carry

    lax.fori_loop(0, qi * kv_per_q, body, 0)
    for t in range(kv_per_q):
        step(qi * kv_per_q + t, t)

    lam = (jnp.exp(jnp.sum(lq1_ref[...] * lk1_ref[...], axis=-1, keepdims=True))
           - jnp.exp(jnp.sum(lq2_ref[...] * lk2_ref[...], axis=-1, keepdims=True)) + lam_init)
    o1 = acc1[...] / jnp.sum(l1[...], axis=-1, keepdims=True)
    o2 = acc2[...] / jnp.sum(l2[...], axis=-1, keepdims=True)
    hd = o1 - lam * o2
    y = _rmsnorm_rows(hd, nw_ref[...]) * (1.0 - lam_init)
    o_ref[...] = y.astype(o_ref.dtype)


def _attn(proj3, lq1, lk1, lq2, lk2, nw, lam_init, tq=2048, tk=512, sub=128):
    B, S, _ = proj3.shape
    w2 = 2 * D_QK_DIM
    lam_spec = pl.BlockSpec((1, D_QK_DIM), lambda b, h, i: (0, 0))
    return pl.pallas_call(
        functools.partial(_attn_kernel, lam_init=lam_init, tq=tq, tk=tk, sub=sub),
        grid=(B, D_HEADS, S // tq),
        in_specs=[
            lam_spec, lam_spec, lam_spec, lam_spec,
            pl.BlockSpec((None, tq, w2), lambda b, h, i: (b, i, OFF_DQ // w2 + h)),
            pl.BlockSpec((None, S, w2), lambda b, h, i: (b, 0, OFF_DK // w2 + h)),
            pl.BlockSpec((None, S, D_V_DIM), lambda b, h, i: (b, 0, OFF_DV // D_V_DIM + h)),
            pl.BlockSpec((1, D_V_DIM), lambda b, h, i: (0, h)),
        ],
        out_specs=pl.BlockSpec((None, tq, D_V_DIM), lambda b, h, i: (b, i, h)),
        out_shape=jax.ShapeDtypeStruct((B, S, D_V), _BF16),
        scratch_shapes=[
            pltpu.VMEM((tq, w2), _BF16),
            pltpu.VMEM((tq, D_V_DIM), _F32),
            pltpu.VMEM((tq, D_V_DIM), _F32),
            pltpu.VMEM((tq, LANES), _F32),
            pltpu.VMEM((tq, LANES), _F32),
            pltpu.VMEM((tq, LANES), _F32),
            pltpu.VMEM((tq, LANES), _F32),
        ],
        compiler_params=_params(("parallel", "parallel", "arbitrary")),
        name="diffattn",
    )(lq1, lk1, lq2, lk2, proj3, proj3, proj3, nw)


def _merge_kernel(hm_ref, hd_ref, wm_ref, wd_ref, gm_ref, gd_ref, y_ref):
    a = _dot(hm_ref[...], wm_ref[...])
    b = _dot(hd_ref[...], wd_ref[...])
    y = _sigmoid(gm_ref[...].astype(_F32)) * a + _sigmoid(gd_ref[...].astype(_F32)) * b
    y_ref[...] = y.astype(y_ref.dtype)


def _merge(hm, hd, wm, wd, proj, layer, tm=1024, tn=1024):
    T = hm.shape[0]
    return pl.pallas_call(
        _merge_kernel,
        grid=(T // tm, D_MODEL // tn),
        in_specs=[
            pl.BlockSpec((tm, M_V), lambda i, j: (i, 0)),
            pl.BlockSpec((tm, D_V), lambda i, j: (i, 0)),
            pl.BlockSpec((None, M_V, tn), lambda i, j: (layer, 0, j)),
            pl.BlockSpec((None, D_V, tn), lambda i, j: (layer, 0, j)),
            pl.BlockSpec((tm, tn), lambda i, j: (i, OFF_GM // tn + j)),
            pl.BlockSpec((tm, tn), lambda i, j: (i, OFF_GD // tn + j)),
        ],
        out_specs=pl.BlockSpec((tm, tn), lambda i, j: (i, j)),
        out_shape=jax.ShapeDtypeStruct((T, D_MODEL), _BF16),
        compiler_params=_params(("parallel", "parallel")),
        name="merge",
    )(hm, hd, wm, wd, proj, proj)


def _resmm_kernel(a_ref, w_ref, x_ref, o_ref):
    o_ref[...] = x_ref[...] + _dot(a_ref[...], w_ref[...])


def _resmm(a, w, x, layer, tm, tn, name):
    T, K = a.shape
    N = w.shape[2]
    return pl.pallas_call(
        _resmm_kernel,
        grid=(T // tm, N // tn),
        in_specs=[
            pl.BlockSpec((tm, K), lambda i, j: (i, 0)),
            pl.BlockSpec((None, K, tn), lambda i, j: (layer, 0, j)),
            pl.BlockSpec((tm, tn), lambda i, j: (i, j)),
        ],
        out_specs=pl.BlockSpec((tm, tn), lambda i, j: (i, j)),
        out_shape=jax.ShapeDtypeStruct((T, N), _F32),
        compiler_params=_params(("parallel", "parallel")),
        name=name,
    )(a, w, x)


def _ffn_up_kernel(x_ref, nw_ref, wg_ref, wu_ref, a_ref, h_scr):
    @pl.when(pl.program_id(1) == 0)
    def _():
        h_scr[...] = _rmsnorm_rows(x_ref[...], nw_ref[...]).astype(_BF16)

    h = h_scr[...]
    g = _dot(h, wg_ref[...])
    u = _dot(h, wu_ref[...])
    a_ref[...] = (g * _sigmoid(g) * u).astype(a_ref.dtype)


def _ffn_up(x, nw, wg, wu, layer, tm=1024, tn=512):
    T = x.shape[0]
    return pl.pallas_call(
        _ffn_up_kernel,
        grid=(T // tm, D_FF // tn),
        in_specs=[
            pl.BlockSpec((tm, D_MODEL), lambda i, j: (i, 0)),
            pl.BlockSpec((1, D_MODEL), lambda i, j: (0, 0)),
            pl.BlockSpec((None, D_MODEL, tn), lambda i, j: (layer, 0, j)),
            pl.BlockSpec((None, D_MODEL, tn), lambda i, j: (layer, 0, j)),
        ],
        out_specs=pl.BlockSpec((tm, tn), lambda i, j: (i, j)),
        out_shape=jax.ShapeDtypeStruct((T, D_FF), _BF16),
        scratch_shapes=[pltpu.VMEM((tm, D_MODEL), _BF16)],
        compiler_params=_params(("parallel", "arbitrary")),
        name="ffn_up",
    )(x, nw, wg, wu)


def _norm_kernel(x_ref, nw_ref, o_ref):
    o_ref[...] = _rmsnorm_rows(x_ref[...], nw_ref[...])


def _final_norm(x, nw, tm=512):
    T = x.shape[0]
    return pl.pallas_call(
        _norm_kernel,
        grid=(T // tm,),
        in_specs=[
            pl.BlockSpec((tm, D_MODEL), lambda i: (i, 0)),
            pl.BlockSpec((1, D_MODEL), lambda i: (0, 0)),
        ],
        out_specs=pl.BlockSpec((tm, D_MODEL), lambda i: (i, 0)),
        out_shape=jax.ShapeDtypeStruct((T, D_MODEL), _F32),
        compiler_params=_params(("parallel",)),
        name="final_norm",
    )(x, nw)


def kernel(x, attn_norm_w, w_in, conv_w, conv_b, b_igate, b_fgate, mlstm_norm_w, lambda_q1, lambda_k1, lambda_q2, lambda_k2, diff_norm_w, w_branch_m, w_branch_d, w_out, ffn_norm_w, w_ffn_gate, w_ffn_up, w_ffn_down, final_norm_w):
    B, S, D = x.shape
    T = B * S
    xf = x.reshape(T, D)
    gate_pad = jnp.zeros((LANES - 2 * M_HEADS,), _F32)
    w_in_t = jnp.swapaxes(w_in, 1, 2)
    w_proj = _cast_win_t(w_in_t)
    w_gate = jnp.pad(w_in_t[:, IN_GATE_LO:IN_GATE_HI, :],
                     ((0, 0), (0, LANES - 2 * M_HEADS), (0, 0))).astype(_BF16)
    w_bm = _cast_stacked(w_branch_m)
    w_bd = _cast_stacked(w_branch_d)
    w_o = _cast_stacked(w_out)
    w_fg = _cast_stacked(w_ffn_gate)
    w_fu = _cast_stacked(w_ffn_up)
    w_fd = _cast_stacked(w_ffn_down)
    for l in range(DEPTH):
        lam_init = 0.8 - 0.6 * math.exp(-0.3 * l)
        gate_bias = jnp.concatenate([b_igate[l], b_fgate[l], gate_pad]).reshape(1, LANES)

        proj, gpre = _inproj(xf, attn_norm_w[l].reshape(1, D), w_proj, w_gate, l)
        gcol, grow = _gates(gpre, gate_bias)
        proj3 = proj.reshape(B, S, N_PROJ)
        hm = _mlstm(proj3, conv_w[l], conv_b[l].reshape(1, 2 * M_QK), gcol.reshape(B, S, LANES), grow,
                    mlstm_norm_w[l].reshape(1, M_V))
        hd = _attn(proj3, lambda_q1[l].reshape(1, D_QK_DIM), lambda_k1[l].reshape(1, D_QK_DIM),
                   lambda_q2[l].reshape(1, D_QK_DIM), lambda_k2[l].reshape(1, D_QK_DIM),
                   diff_norm_w[l].reshape(1, D_V), lam_init)
        y = _merge(hm.reshape(T, M_V), hd.reshape(T, D_V), w_bm, w_bd, proj, l)
        xf = _resmm(y, w_o, xf, l, 1024, 1024, "out_proj")
        a = _ffn_up(xf, ffn_norm_w[l].reshape(1, D), w_fg, w_fu, l)
        xf = _resmm(a, w_fd, xf, l, 1024, 512, "ffn_down")
    out = _final_norm(xf, final_norm_w.reshape(1, D))
    return out.reshape(B, S, D)
```

```python
import functools
import math

import jax
import jax.numpy as jnp
from jax import lax
from jax.experimental import pallas as pl
from jax.experimental.pallas import tpu as pltpu

D_MODEL = 2048
DEPTH = 4
M_HEADS = 8
M_QK_DIM = 128
M_V_DIM = 256
CONV_K = 4
M_QK = M_HEADS * M_QK_DIM
M_V = M_HEADS * M_V_DIM
D_HEADS = 8
D_QK_DIM = 128
D_V_DIM = 256
D_Q = D_HEADS * 2 * D_QK_DIM
D_V = D_HEADS * D_V_DIM
D_FF = 5632
EPS = 1e-6

IN_GATE_LO = 2 * M_QK + 2 * M_V
IN_GATE_HI = IN_GATE_LO + 2 * M_HEADS
OFF_MQ = 0
OFF_MK = M_QK
OFF_MV = 2 * M_QK
OFF_MO = OFF_MV + M_V
OFF_DQ = OFF_MO + M_V
OFF_DK = OFF_DQ + D_Q
OFF_DV = OFF_DK + D_Q
OFF_GM = OFF_DV + D_V
OFF_GD = OFF_GM + D_MODEL
N_PROJ = OFF_GD + D_MODEL

LANES = 128
SUBLANES = 8
M_CHUNK = 256
GATE_ROWS = 1024
VMEM_LIMIT = 56 * 1024 * 1024

_F32 = jnp.float32
_BF16 = jnp.bfloat16


def _dot(a, b):
    return jnp.dot(a, b, preferred_element_type=_F32)


def _dot_nt(a, b):
    return lax.dot_general(a, b, (((1,), (1,)), ((), ())), preferred_element_type=_F32)


def _dot_tn(a, b):
    return lax.dot_general(a, b, (((0,), (0,)), ((), ())), preferred_element_type=_F32)


def _sigmoid(x):
    return 1.0 / (1.0 + jnp.exp(-x))


def _rmsnorm_rows(x, w):
    ms = jnp.mean(x * x, axis=-1, keepdims=True)
    return x * lax.rsqrt(ms + EPS) * w


def _params(sem):
    return pltpu.CompilerParams(dimension_semantics=sem, vmem_limit_bytes=VMEM_LIMIT)


def _cast_kernel(w_ref, o_ref):
    o_ref[...] = w_ref[...].astype(o_ref.dtype)


def _cast_stacked(w, rows=256):
    L, K, N = w.shape
    return pl.pallas_call(
        _cast_kernel,
        grid=(L, K // rows),
        in_specs=[pl.BlockSpec((None, rows, N), lambda l, i: (l, i, 0))],
        out_specs=pl.BlockSpec((None, rows, N), lambda l, i: (l, i, 0)),
        out_shape=jax.ShapeDtypeStruct((L, K, N), _BF16),
        compiler_params=_params(("parallel", "parallel")),
        name="cast",
    )(w)


def _inproj_kernel(x_ref, nw_ref, w_ref, wg_ref, o_ref, g_ref, h_scr):
    @pl.when(pl.program_id(1) == 0)
    def _():
        hb = _rmsnorm_rows(x_ref[...], nw_ref[...]).astype(_BF16)
        h_scr[...] = hb
        g_ref[...] = _dot_nt(hb, wg_ref[0].astype(_BF16))

    o_ref[...] = _dot_nt(h_scr[...], w_ref[0].astype(_BF16)).astype(o_ref.dtype)


def _inproj(x, nw, w_in_t, layer, tm=1024, tn=1024):
    T = x.shape[0]
    n_lo = IN_GATE_LO // tn

    def w_rows(i, j):
        start = jnp.where(j < n_lo, j * tn, IN_GATE_HI + (j - n_lo) * tn)
        return (layer, pl.multiple_of(start, 2 * M_HEADS), 0)

    el = pl.Element
    return pl.pallas_call(
        _inproj_kernel,
        grid=(T // tm, N_PROJ // tn),
        in_specs=[
            pl.BlockSpec((tm, D_MODEL), lambda i, j: (i, 0)),
            pl.BlockSpec((1, D_MODEL), lambda i, j: (0, 0)),
            pl.BlockSpec((el(1), el(tn), el(D_MODEL)), w_rows),
            pl.BlockSpec((el(1), el(LANES), el(D_MODEL)), lambda i, j: (layer, IN_GATE_LO, 0)),
        ],
        out_specs=[
            pl.BlockSpec((tm, tn), lambda i, j: (i, j)),
            pl.BlockSpec((tm, LANES), lambda i, j: (i, 0)),
        ],
        out_shape=[
            jax.ShapeDtypeStruct((T, N_PROJ), _BF16),
            jax.ShapeDtypeStruct((T, LANES), _F32),
        ],
        scratch_shapes=[pltpu.VMEM((tm, D_MODEL), _BF16)],
        compiler_params=_params(("parallel", "arbitrary")),
        name="inproj",
    )(x, nw, w_in_t, w_in_t)


def _gates_kernel(g_ref, b_ref, gcol_ref, grow_ref):
    z = g_ref[...] + b_ref[...]
    rows = z.shape[0]
    lane = lax.broadcasted_iota(jnp.int32, z.shape, 1)
    lf = jnp.minimum(z, 0.0) - jnp.log1p(jnp.exp(-jnp.abs(z)))
    pos = lax.broadcasted_iota(jnp.int32, z.shape, 0) % M_CHUNK
    csum = lf
    d = 1
    while d < M_CHUNK:
        csum = csum + jnp.where(pos >= d, pltpu.roll(csum, d, 0), 0.0)
        d *= 2
    is_f = (lane >= M_HEADS) & (lane < 2 * M_HEADS)
    col = jnp.where(is_f, csum, z)
    gcol_ref[...] = col
    tr = col.T
    grow_ref[...] = tr[0:M_HEADS, :] - tr[M_HEADS:2 * M_HEADS, :]


def _gates(gpre, bias):
    T = gpre.shape[0]
    return pl.pallas_call(
        _gates_kernel,
        grid=(T // GATE_ROWS,),
        in_specs=[
            pl.BlockSpec((GATE_ROWS, LANES), lambda i: (i, 0)),
            pl.BlockSpec((1, LANES), lambda i: (0, 0)),
        ],
        out_specs=[
            pl.BlockSpec((GATE_ROWS, LANES), lambda i: (i, 0)),
            pl.BlockSpec((M_HEADS, GATE_ROWS), lambda i: (0, i)),
        ],
        out_shape=[
            jax.ShapeDtypeStruct((T, LANES), _F32),
            jax.ShapeDtypeStruct((M_HEADS, T), _F32),
        ],
        compiler_params=_params(("parallel",)),
        name="gates",
    )(gpre, bias)


def _mlstm_kernel(q_ref, k_ref, v_ref, o_ref, cwq_ref, cbq_ref, cwk_ref, cbk_ref,
                  gcol_ref, grow_ref, nw_ref, out_ref, xq_scr, xk_scr, c_scr, m_scr):
    head = pl.program_id(1)
    S = q_ref.shape[0]
    L = M_CHUNK
    pad = SUBLANES
    zpad = jnp.zeros((pad, M_QK_DIM), _F32)
    xq_scr[0:pad, :] = zpad
    xk_scr[0:pad, :] = zpad
    xq_scr[pad:, :] = q_ref[...].astype(_F32)
    xk_scr[pad:, :] = k_ref[...].astype(_F32)
    c_scr[...] = jnp.zeros(c_scr.shape, _F32)
    m_scr[...] = jnp.zeros(m_scr.shape, _F32)

    lane = lax.broadcasted_iota(jnp.int32, (L, LANES), 1)
    row_i = lax.broadcasted_iota(jnp.int32, (L, L), 0)
    col_i = lax.broadcasted_iota(jnp.int32, (L, L), 1)
    tril = col_i <= row_i
    ones_aug = jnp.ones((L, LANES), _BF16)
    k_scale = M_QK_DIM ** -0.5

    def conv_silu(x_scr, cw_ref, cb_ref, r0):
        acc = cb_ref[...] + cw_ref[CONV_K - 1:CONV_K, :] * x_scr[pl.ds(r0 + pad, L), :]
        for j in range(CONV_K - 1):
            acc = acc + cw_ref[j:j + 1, :] * x_scr[pl.ds(r0 + pad - (CONV_K - 1 - j), L), :]
        return acc * _sigmoid(acc)

    def chunk(c, carry):
        r0 = pl.multiple_of(c * L, L)
        qf = conv_silu(xq_scr, cwq_ref, cbq_ref, r0)
        kf = conv_silu(xk_scr, cwk_ref, cbk_ref, r0) * k_scale
        qb = qf.astype(_BF16)
        kb = kf.astype(_BF16)
        vaug = jnp.concatenate([v_ref[pl.ds(r0, L), :], ones_aug], axis=1)

        g = gcol_ref[pl.ds(r0, L), :]
        ig = jnp.sum(jnp.where(lane == head, g, 0.0), axis=-1, keepdims=True)
        b = jnp.sum(jnp.where(lane == head + M_HEADS, g, 0.0), axis=-1, keepdims=True)
        r = grow_ref[pl.ds(head, 1), pl.ds(r0, L)]
        m_prev = m_scr[...]

        log_d = jnp.where(tril, b + r, -jnp.inf)
        log_inter = b + m_prev
        m_row = jnp.maximum(log_inter, jnp.max(log_d, axis=-1, keepdims=True))
        dmat = jnp.exp(log_d - m_row)
        inter = jnp.exp(log_inter - m_row)
        s = _dot_nt(qb, kb) * dmat
        num_aug = _dot(s.astype(_BF16), vaug) + inter * _dot(qb, c_scr[...].astype(_BF16))
        num = num_aug[:, :M_V_DIM]
        den = num_aug[:, M_V_DIM:M_V_DIM + 1]
        hh = num / jnp.maximum(jnp.abs(den), jnp.exp(-m_row))
        y = _rmsnorm_rows(hh, nw_ref[...])
        gate = _sigmoid(o_ref[pl.ds(r0, L), :].astype(_F32))
        out_ref[pl.ds(r0, L), :] = (gate * y).astype(out_ref.dtype)

        b_last = b[L - 1:L, :]
        log_w = b_last - b + ig
        m_new = jnp.maximum(b_last + m_prev, jnp.max(log_w, axis=0, keepdims=True))
        w = jnp.exp(log_w - m_new)
        decay = jnp.exp(b_last + m_prev - m_new)
        kw = (kf * w).astype(_BF16)
        c_scr[...] = decay * c_scr[...] + _dot_tn(kw, vaug)
        m_scr[...] = m_new
        return carry

    lax.fori_loop(0, S // L, chunk, 0)


def _mlstm(proj3, conv_w, conv_b, gcol3, grow, nw):
    B, S, _ = proj3.shape
    qk_blk = lambda off: (lambda b, h: (b, 0, off // M_QK_DIM + h))
    v_blk = lambda off: (lambda b, h: (b, 0, off // M_V_DIM + h))
    return pl.pallas_call(
        _mlstm_kernel,
        grid=(B, M_HEADS),
        in_specs=[
            pl.BlockSpec((None, S, M_QK_DIM), qk_blk(OFF_MQ)),
            pl.BlockSpec((None, S, M_QK_DIM), qk_blk(OFF_MK)),
            pl.BlockSpec((None, S, M_V_DIM), v_blk(OFF_MV)),
            pl.BlockSpec((None, S, M_V_DIM), v_blk(OFF_MO)),
            pl.BlockSpec((CONV_K, M_QK_DIM), lambda b, h: (0, h)),
            pl.BlockSpec((1, M_QK_DIM), lambda b, h: (0, h)),
            pl.BlockSpec((CONV_K, M_QK_DIM), lambda b, h: (0, M_HEADS + h)),
            pl.BlockSpec((1, M_QK_DIM), lambda b, h: (0, M_HEADS + h)),
            pl.BlockSpec((None, S, LANES), lambda b, h: (b, 0, 0)),
            pl.BlockSpec((M_HEADS, S), lambda b, h: (0, b)),
            pl.BlockSpec((1, M_V_DIM), lambda b, h: (0, h)),
        ],
        out_specs=pl.BlockSpec((None, S, M_V_DIM), lambda b, h: (b, 0, h)),
        out_shape=jax.ShapeDtypeStruct((B, S, M_V), _BF16),
        scratch_shapes=[
            pltpu.VMEM((S + SUBLANES, M_QK_DIM), _F32),
            pltpu.VMEM((S + SUBLANES, M_QK_DIM), _F32),
            pltpu.VMEM((M_QK_DIM, M_V_DIM + LANES), _F32),
            pltpu.VMEM((1, 1), _F32),
        ],
        compiler_params=_params(("parallel", "parallel")),
        name="mlstm",
    )(proj3, proj3, proj3, proj3, conv_w, conv_b, conv_w, conv_b, gcol3, grow, nw)


def _attn_kernel(lq1_ref, lk1_ref, lq2_ref, lk2_ref, q_ref, k_ref, v_ref, nw_ref, o_ref,
                 qs_scr, acc1, acc2, m1, l1, m2, l2, *, lam_init, tq, tk, sub):
    qi = pl.program_id(2)
    d = D_QK_DIM
    qs_scr[...] = (q_ref[...].astype(_F32) * ((d ** -0.5) * math.log2(math.e))).astype(_BF16)
    n_lane_blk = tk // LANES
    kv_per_q = tq // tk
    streams = ((acc1, m1, l1), (acc2, m2, l2))
    for acc, m, l in streams:
        acc[...] = jnp.zeros(acc.shape, _F32)
        m[...] = jnp.full(m.shape, -jnp.inf, _F32)
        l[...] = jnp.zeros(l.shape, _F32)

    row_i = lax.broadcasted_iota(jnp.int32, (sub, LANES), 0)
    col_i = lax.broadcasted_iota(jnp.int32, (sub, LANES), 1)

    def step(j, diag):
        r0 = pl.multiple_of(j * tk, tk)
        for idx, (acc, m, l) in enumerate(streams):
            kk = k_ref[pl.ds(r0, tk), idx * d:(idx + 1) * d]
            vv = v_ref[pl.ds(r0, tk), :]
            for r in range(tq // sub):
                if diag is not None and (r + 1) * sub - 1 < diag * tk:
                    continue
                rows = pl.ds(r * sub, sub)
                s = _dot_nt(qs_scr[rows, idx * d:(idx + 1) * d], kk)
                blocks = [s[:, b * LANES:(b + 1) * LANES] for b in range(n_lane_blk)]
                if diag is not None:
                    blocks = [blk if diag * tk + (b + 1) * LANES - 1 <= r * sub else
                              jnp.where(col_i + (diag * tk + b * LANES) <= row_i + r * sub, blk, -jnp.inf)
                              for b, blk in enumerate(blocks)]
                m_cur = blocks[0]
                for blk in blocks[1:]:
                    m_cur = jnp.maximum(m_cur, blk)
                m_prev = m[rows, :]
                m_new = jnp.maximum(m_prev, jnp.max(m_cur, axis=-1, keepdims=True))
                alpha = jnp.exp2(m_prev - m_new)
                ps = [jnp.exp2(blk - m_new) for blk in blocks]
                l_add = ps[0]
                for pb in ps[1:]:
                    l_add = l_add + pb
                l[rows, :] = alpha * l[rows, :] + l_add
                p = jnp.concatenate([pb.astype(_BF16) for pb in ps], axis=1)
                alpha2 = jnp.concatenate([alpha] * (D_V_DIM // LANES), axis=1)
                acc[rows, :] = alpha2 * acc[rows, :] + _dot(p, vv)
                m[rows, :] = m_new

    def body(j, carry):
        step(j, None)
        return carry

    lax.fori_loop(0, qi * kv_per_q, body, 0)
    for t in range(kv_per_q):
        step(qi * kv_per_q + t, t)

    lam = (jnp.exp(jnp.sum(lq1_ref[...] * lk1_ref[...], axis=-1, keepdims=True))
           - jnp.exp(jnp.sum(lq2_ref[...] * lk2_ref[...], axis=-1, keepdims=True)) + lam_init)
    o1 = acc1[...] / jnp.sum(l1[...], axis=-1, keepdims=True)
    o2 = acc2[...] / jnp.sum(l2[...], axis=-1, keepdims=True)
    hd = o1 - lam * o2
    y = _rmsnorm_rows(hd, nw_ref[...]) * (1.0 - lam_init)
    o_ref[...] = y.astype(o_ref.dtype)


def _attn(proj3, lq1, lk1, lq2, lk2, nw, lam_init, tq=2048, tk=512, sub=128):
    B, S, _ = proj3.shape
    w2 = 2 * D_QK_DIM
    lam_spec = pl.BlockSpec((1, D_QK_DIM), lambda b, h, i: (0, 0))
    return pl.pallas_call(
        functools.partial(_attn_kernel, lam_init=lam_init, tq=tq, tk=tk, sub=sub),
        grid=(B, D_HEADS, S // tq),
        in_specs=[
            lam_spec, lam_spec, lam_spec, lam_spec,
            pl.BlockSpec((None, tq, w2), lambda b, h, i: (b, i, OFF_DQ // w2 + h)),
            pl.BlockSpec((None, S, w2), lambda b, h, i: (b, 0, OFF_DK // w2 + h)),
            pl.BlockSpec((None, S, D_V_DIM), lambda b, h, i: (b, 0, OFF_DV // D_V_DIM + h)),
            pl.BlockSpec((1, D_V_DIM), lambda b, h, i: (0, h)),
        ],
        out_specs=pl.BlockSpec((None, tq, D_V_DIM), lambda b, h, i: (b, i, h)),
        out_shape=jax.ShapeDtypeStruct((B, S, D_V), _BF16),
        scratch_shapes=[
            pltpu.VMEM((tq, w2), _BF16),
            pltpu.VMEM((tq, D_V_DIM), _F32),
            pltpu.VMEM((tq, D_V_DIM), _F32),
            pltpu.VMEM((tq, LANES), _F32),
            pltpu.VMEM((tq, LANES), _F32),
            pltpu.VMEM((tq, LANES), _F32),
            pltpu.VMEM((tq, LANES), _F32),
        ],
        compiler_params=_params(("parallel", "parallel", "arbitrary")),
        name="diffattn",
    )(lq1, lk1, lq2, lk2, proj3, proj3, proj3, nw)


def _merge_kernel(hm_ref, hd_ref, wm_ref, wd_ref, gm_ref, gd_ref, y_ref):
    a = _dot(hm_ref[...], wm_ref[...])
    b = _dot(hd_ref[...], wd_ref[...])
    y = _sigmoid(gm_ref[...].astype(_F32)) * a + _sigmoid(gd_ref[...].astype(_F32)) * b
    y_ref[...] = y.astype(y_ref.dtype)


def _merge(hm, hd, wm, wd, proj, layer, tm=1024, tn=1024):
    T = hm.shape[0]
    return pl.pallas_call(
        _merge_kernel,
        grid=(T // tm, D_MODEL // tn),
        in_specs=[
            pl.BlockSpec((tm, M_V), lambda i, j: (i, 0)),
            pl.BlockSpec((tm, D_V), lambda i, j: (i, 0)),
            pl.BlockSpec((None, M_V, tn), lambda i, j: (layer, 0, j)),
            pl.BlockSpec((None, D_V, tn), lambda i, j: (layer, 0, j)),
            pl.BlockSpec((tm, tn), lambda i, j: (i, OFF_GM // tn + j)),
            pl.BlockSpec((tm, tn), lambda i, j: (i, OFF_GD // tn + j)),
        ],
        out_specs=pl.BlockSpec((tm, tn), lambda i, j: (i, j)),
        out_shape=jax.ShapeDtypeStruct((T, D_MODEL), _BF16),
        compiler_params=_params(("parallel", "parallel")),
        name="merge",
    )(hm, hd, wm, wd, proj, proj)


def _resmm_kernel(a_ref, w_ref, x_ref, o_ref):
    o_ref[...] = x_ref[...] + _dot(a_ref[...], w_ref[...])


def _resmm(a, w, x, layer, tm, tn, name):
    T, K = a.shape
    N = w.shape[2]
    return pl.pallas_call(
        _resmm_kernel,
        grid=(T // tm, N // tn),
        in_specs=[
            pl.BlockSpec((tm, K), lambda i, j: (i, 0)),
            pl.BlockSpec((None, K, tn), lambda i, j: (layer, 0, j)),
            pl.BlockSpec((tm, tn), lambda i, j: (i, j)),
        ],
        out_specs=pl.BlockSpec((tm, tn), lambda i, j: (i, j)),
        out_shape=jax.ShapeDtypeStruct((T, N), _F32),
        compiler_params=_params(("parallel", "parallel")),
        name=name,
    )(a, w, x)


def _ffn_up_kernel(x_ref, nw_ref, wg_ref, wu_ref, a_ref, h_scr):
    @pl.when(pl.program_id(1) == 0)
    def _():
        h_scr[...] = _rmsnorm_rows(x_ref[...], nw_ref[...]).astype(_BF16)

    h = h_scr[...]
    g = _dot(h, wg_ref[...].astype(_BF16))
    u = _dot(h, wu_ref[...].astype(_BF16))
    a_ref[...] = (g * _sigmoid(g) * u).astype(a_ref.dtype)


def _ffn_up(x, nw, wg, wu, layer, tm=1024, tn=512):
    T = x.shape[0]
    return pl.pallas_call(
        _ffn_up_kernel,
        grid=(T // tm, D_FF // tn),
        in_specs=[
            pl.BlockSpec((tm, D_MODEL), lambda i, j: (i, 0)),
            pl.BlockSpec((1, D_MODEL), lambda i, j: (0, 0)),
            pl.BlockSpec((None, D_MODEL, tn), lambda i, j: (layer, 0, j)),
            pl.BlockSpec((None, D_MODEL, tn), lambda i, j: (layer, 0, j)),
        ],
        out_specs=pl.BlockSpec((tm, tn), lambda i, j: (i, j)),
        out_shape=jax.ShapeDtypeStruct((T, D_FF), _BF16),
        scratch_shapes=[pltpu.VMEM((tm, D_MODEL), _BF16)],
        compiler_params=_params(("parallel", "arbitrary")),
        name="ffn_up",
    )(x, nw, wg, wu)


def _norm_kernel(x_ref, nw_ref, o_ref):
    o_ref[...] = _rmsnorm_rows(x_ref[...], nw_ref[...])


def _final_norm(x, nw, tm=512):
    T = x.shape[0]
    return pl.pallas_call(
        _norm_kernel,
        grid=(T // tm,),
        in_specs=[
            pl.BlockSpec((tm, D_MODEL), lambda i: (i, 0)),
            pl.BlockSpec((1, D_MODEL), lambda i: (0, 0)),
        ],
        out_specs=pl.BlockSpec((tm, D_MODEL), lambda i: (i, 0)),
        out_shape=jax.ShapeDtypeStruct((T, D_MODEL), _F32),
        compiler_params=_params(("parallel",)),
        name="final_norm",
    )(x, nw)


def kernel(x, attn_norm_w, w_in, conv_w, conv_b, b_igate, b_fgate, mlstm_norm_w, lambda_q1, lambda_k1, lambda_q2, lambda_k2, diff_norm_w, w_branch_m, w_branch_d, w_out, ffn_norm_w, w_ffn_gate, w_ffn_up, w_ffn_down, final_norm_w):
    B, S, D = x.shape
    T = B * S
    xf = x.reshape(T, D)
    gate_pad = jnp.zeros((LANES - 2 * M_HEADS,), _F32)
    w_in_t = jnp.swapaxes(w_in, 1, 2)
    w_bm = _cast_stacked(w_branch_m)
    w_bd = _cast_stacked(w_branch_d)
    w_o = _cast_stacked(w_out)
    w_fd = _cast_stacked(w_ffn_down)
    for l in range(DEPTH):
        lam_init = 0.8 - 0.6 * math.exp(-0.3 * l)
        gate_bias = jnp.concatenate([b_igate[l], b_fgate[l], gate_pad]).reshape(1, LANES)

        proj, gpre = _inproj(xf, attn_norm_w[l].reshape(1, D), w_in_t, l)
        gcol, grow = _gates(gpre, gate_bias)
        proj3 = proj.reshape(B, S, N_PROJ)
        hm = _mlstm(proj3, conv_w[l], conv_b[l].reshape(1, 2 * M_QK), gcol.reshape(B, S, LANES), grow,
                    mlstm_norm_w[l].reshape(1, M_V))
        hd = _attn(proj3, lambda_q1[l].reshape(1, D_QK_DIM), lambda_k1[l].reshape(1, D_QK_DIM),
                   lambda_q2[l].reshape(1, D_QK_DIM), lambda_k2[l].reshape(1, D_QK_DIM),
                   diff_norm_w[l].reshape(1, D_V), lam_init)
        y = _merge(hm.reshape(T, M_V), hd.reshape(T, D_V), w_bm, w_bd, proj, l)
        xf = _resmm(y, w_o, xf, l, 1024, 1024, "out_proj")
        a = _ffn_up(xf, ffn_norm_w[l].reshape(1, D), w_ffn_gate, w_ffn_up, l)
        xf = _resmm(a, w_fd, xf, l, 1024, 512, "ffn_down")
    out = _final_norm(xf, final_norm_w.reshape(1, D))
    return out.reshape(B, S, D)
```

```python
import functools
import math

import jax
import jax.numpy as jnp
from jax import lax
from jax.experimental import pallas as pl
from jax.experimental.pallas import tpu as pltpu

D_MODEL = 2048
DEPTH = 4
M_HEADS = 8
M_QK_DIM = 128
M_V_DIM = 256
CONV_K = 4
M_QK = M_HEADS * M_QK_DIM
M_V = M_HEADS * M_V_DIM
D_HEADS = 8
D_QK_DIM = 128
D_V_DIM = 256
D_Q = D_HEADS * 2 * D_QK_DIM
D_V = D_HEADS * D_V_DIM
D_FF = 5632
EPS = 1e-6

IN_GATE_LO = 2 * M_QK + 2 * M_V
IN_GATE_HI = IN_GATE_LO + 2 * M_HEADS
OFF_MQ = 0
OFF_MK = M_QK
OFF_MV = 2 * M_QK
OFF_MO = OFF_MV + M_V
OFF_DQ = OFF_MO + M_V
OFF_DK = OFF_DQ + D_Q
OFF_DV = OFF_DK + D_Q
OFF_GM = OFF_DV + D_V
OFF_GD = OFF_GM + D_MODEL
N_PROJ = OFF_GD + D_MODEL

LANES = 128
SUBLANES = 8
M_CHUNK = 256
GATE_ROWS = 1024
VMEM_LIMIT = 56 * 1024 * 1024

_F32 = jnp.float32
_BF16 = jnp.bfloat16


def _dot(a, b):
    return jnp.dot(a, b, preferred_element_type=_F32)


def _dot_nt(a, b):
    return lax.dot_general(a, b, (((1,), (1,)), ((), ())), preferred_element_type=_F32)


def _dot_tn(a, b):
    return lax.dot_general(a, b, (((0,), (0,)), ((), ())), preferred_element_type=_F32)


def _sigmoid(x):
    return 1.0 / (1.0 + jnp.exp(-x))


def _rmsnorm_rows(x, w):
    ms = jnp.mean(x * x, axis=-1, keepdims=True)
    return x * lax.rsqrt(ms + EPS) * w


def _params(sem):
    return pltpu.CompilerParams(dimension_semantics=sem, vmem_limit_bytes=VMEM_LIMIT)


def _inproj_kernel(x_ref, nw_ref, w_ref, wg_ref, o_ref, g_ref, h_scr):
    @pl.when(pl.program_id(1) == 0)
    def _():
        hb = _rmsnorm_rows(x_ref[...], nw_ref[...]).astype(_BF16)
        h_scr[...] = hb
        g_ref[...] = _dot_nt(hb, wg_ref[0].astype(_BF16))

    o_ref[...] = _dot_nt(h_scr[...], w_ref[0].astype(_BF16)).astype(o_ref.dtype)


def _inproj(x, nw, w_in_t, layer, tm=1024, tn=1024):
    T = x.shape[0]
    n_lo = IN_GATE_LO // tn

    def w_rows(i, j):
        start = jnp.where(j < n_lo, j * tn, IN_GATE_HI + (j - n_lo) * tn)
        return (layer, pl.multiple_of(start, 2 * M_HEADS), 0)

    el = pl.Element
    return pl.pallas_call(
        _inproj_kernel,
        grid=(T // tm, N_PROJ // tn),
        in_specs=[
            pl.BlockSpec((tm, D_MODEL), lambda i, j: (i, 0)),
            pl.BlockSpec((1, D_MODEL), lambda i, j: (0, 0)),
            pl.BlockSpec((el(1), el(tn), el(D_MODEL)), w_rows),
            pl.BlockSpec((el(1), el(LANES), el(D_MODEL)), lambda i, j: (layer, IN_GATE_LO, 0)),
        ],
        out_specs=[
            pl.BlockSpec((tm, tn), lambda i, j: (i, j)),
            pl.BlockSpec((tm, LANES), lambda i, j: (i, 0)),
        ],
        out_shape=[
            jax.ShapeDtypeStruct((T, N_PROJ), _BF16),
            jax.ShapeDtypeStruct((T, LANES), _F32),
        ],
        scratch_shapes=[pltpu.VMEM((tm, D_MODEL), _BF16)],
        compiler_params=_params(("parallel", "arbitrary")),
        name="inproj",
    )(x, nw, w_in_t, w_in_t)


def _gates_kernel(g_ref, b_ref, gcol_ref, grow_ref):
    z = g_ref[...] + b_ref[...]
    rows = z.shape[0]
    lane = lax.broadcasted_iota(jnp.int32, z.shape, 1)
    lf = jnp.minimum(z, 0.0) - jnp.log1p(jnp.exp(-jnp.abs(z)))
    pos = lax.broadcasted_iota(jnp.int32, z.shape, 0) % M_CHUNK
    csum = lf
    d = 1
    while d < M_CHUNK:
        csum = csum + jnp.where(pos >= d, pltpu.roll(csum, d, 0), 0.0)
        d *= 2
    is_f = (lane >= M_HEADS) & (lane < 2 * M_HEADS)
    col = jnp.where(is_f, csum, z)
    gcol_ref[...] = col
    tr = col.T
    grow_ref[...] = tr[0:M_HEADS, :] - tr[M_HEADS:2 * M_HEADS, :]


def _gates(gpre, bias):
    T = gpre.shape[0]
    return pl.pallas_call(
        _gates_kernel,
        grid=(T // GATE_ROWS,),
        in_specs=[
            pl.BlockSpec((GATE_ROWS, LANES), lambda i: (i, 0)),
            pl.BlockSpec((1, LANES), lambda i: (0, 0)),
        ],
        out_specs=[
            pl.BlockSpec((GATE_ROWS, LANES), lambda i: (i, 0)),
            pl.BlockSpec((M_HEADS, GATE_ROWS), lambda i: (0, i)),
        ],
        out_shape=[
            jax.ShapeDtypeStruct((T, LANES), _F32),
            jax.ShapeDtypeStruct((M_HEADS, T), _F32),
        ],
        compiler_params=_params(("parallel",)),
        name="gates",
    )(gpre, bias)


def _mlstm_kernel(q_ref, k_ref, v_ref, o_ref, cwq_ref, cbq_ref, cwk_ref, cbk_ref,
                  gcol_ref, grow_ref, nw_ref, wcast_ref, out_ref, wcast_o, xq_scr, xk_scr, c_scr, m_scr):
    wcast_o[...] = wcast_ref[...].astype(wcast_o.dtype)
    head = pl.program_id(1)
    S = q_ref.shape[0]
    L = M_CHUNK
    pad = SUBLANES
    zpad = jnp.zeros((pad, M_QK_DIM), _F32)
    xq_scr[0:pad, :] = zpad
    xk_scr[0:pad, :] = zpad
    xq_scr[pad:, :] = q_ref[...].astype(_F32)
    xk_scr[pad:, :] = k_ref[...].astype(_F32)
    c_scr[...] = jnp.zeros(c_scr.shape, _F32)
    m_scr[...] = jnp.zeros(m_scr.shape, _F32)

    lane = lax.broadcasted_iota(jnp.int32, (L, LANES), 1)
    row_i = lax.broadcasted_iota(jnp.int32, (L, L), 0)
    col_i = lax.broadcasted_iota(jnp.int32, (L, L), 1)
    tril = col_i <= row_i
    ones_aug = jnp.ones((L, LANES), _BF16)
    k_scale = M_QK_DIM ** -0.5

    def conv_silu(x_scr, cw_ref, cb_ref, r0):
        acc = cb_ref[...] + cw_ref[CONV_K - 1:CONV_K, :] * x_scr[pl.ds(r0 + pad, L), :]
        for j in range(CONV_K - 1):
            acc = acc + cw_ref[j:j + 1, :] * x_scr[pl.ds(r0 + pad - (CONV_K - 1 - j), L), :]
        return acc * _sigmoid(acc)

    def chunk(c, carry):
        r0 = pl.multiple_of(c * L, L)
        qf = conv_silu(xq_scr, cwq_ref, cbq_ref, r0)
        kf = conv_silu(xk_scr, cwk_ref, cbk_ref, r0) * k_scale
        qb = qf.astype(_BF16)
        kb = kf.astype(_BF16)
        vaug = jnp.concatenate([v_ref[pl.ds(r0, L), :], ones_aug], axis=1)

        g = gcol_ref[pl.ds(r0, L), :]
        ig = jnp.sum(jnp.where(lane == head, g, 0.0), axis=-1, keepdims=True)
        b = jnp.sum(jnp.where(lane == head + M_HEADS, g, 0.0), axis=-1, keepdims=True)
        r = grow_ref[pl.ds(head, 1), pl.ds(r0, L)]
        m_prev = m_scr[...]

        log_d = jnp.where(tril, b + r, -jnp.inf)
        log_inter = b + m_prev
        m_row = jnp.maximum(log_inter, jnp.max(log_d, axis=-1, keepdims=True))
        dmat = jnp.exp(log_d - m_row)
        inter = jnp.exp(log_inter - m_row)
        s = _dot_nt(qb, kb) * dmat
        num_aug = _dot(s.astype(_BF16), vaug) + inter * _dot(qb, c_scr[...].astype(_BF16))
        num = num_aug[:, :M_V_DIM]
        den = num_aug[:, M_V_DIM:M_V_DIM + 1]
        hh = num / jnp.maximum(jnp.abs(den), jnp.exp(-m_row))
        y = _rmsnorm_rows(hh, nw_ref[...])
        gate = _sigmoid(o_ref[pl.ds(r0, L), :].astype(_F32))
        out_ref[pl.ds(r0, L), :] = (gate * y).astype(out_ref.dtype)

        b_last = b[L - 1:L, :]
        log_w = b_last - b + ig
        m_new = jnp.maximum(b_last + m_prev, jnp.max(log_w, axis=0, keepdims=True))
        w = jnp.exp(log_w - m_new)
        decay = jnp.exp(b_last + m_prev - m_new)
        kw = (kf * w).astype(_BF16)
        c_scr[...] = decay * c_scr[...] + _dot_tn(kw, vaug)
        m_scr[...] = m_new
        return carry

    lax.fori_loop(0, S // L, chunk, 0)


def _mlstm(proj3, conv_w, conv_b, gcol3, grow, nw, w_cast, layer):
    B, S, _ = proj3.shape
    _, wk, wn = w_cast.shape
    slab = wk // (B * M_HEADS)
    qk_blk = lambda off: (lambda b, h: (b, 0, off // M_QK_DIM + h))
    v_blk = lambda off: (lambda b, h: (b, 0, off // M_V_DIM + h))
    return pl.pallas_call(
        _mlstm_kernel,
        grid=(B, M_HEADS),
        in_specs=[
            pl.BlockSpec((None, S, M_QK_DIM), qk_blk(OFF_MQ)),
            pl.BlockSpec((None, S, M_QK_DIM), qk_blk(OFF_MK)),
            pl.BlockSpec((None, S, M_V_DIM), v_blk(OFF_MV)),
            pl.BlockSpec((None, S, M_V_DIM), v_blk(OFF_MO)),
            pl.BlockSpec((CONV_K, M_QK_DIM), lambda b, h: (0, h)),
            pl.BlockSpec((1, M_QK_DIM), lambda b, h: (0, h)),
            pl.BlockSpec((CONV_K, M_QK_DIM), lambda b, h: (0, M_HEADS + h)),
            pl.BlockSpec((1, M_QK_DIM), lambda b, h: (0, M_HEADS + h)),
            pl.BlockSpec((None, S, LANES), lambda b, h: (b, 0, 0)),
            pl.BlockSpec((M_HEADS, S), lambda b, h: (0, b)),
            pl.BlockSpec((1, M_V_DIM), lambda b, h: (0, h)),
            pl.BlockSpec((None, slab, wn), lambda b, h: (layer, b * M_HEADS + h, 0)),
        ],
        out_specs=[pl.BlockSpec((None, S, M_V_DIM), lambda b, h: (b, 0, h)),
                   pl.BlockSpec((slab, wn), lambda b, h: (b * M_HEADS + h, 0))],
        out_shape=[jax.ShapeDtypeStruct((B, S, M_V), _BF16), jax.ShapeDtypeStruct((wk, wn), _BF16)],
        scratch_shapes=[
            pltpu.VMEM((S + SUBLANES, M_QK_DIM), _F32),
            pltpu.VMEM((S + SUBLANES, M_QK_DIM), _F32),
            pltpu.VMEM((M_QK_DIM, M_V_DIM + LANES), _F32),
            pltpu.VMEM((1, 1), _F32),
        ],
        compiler_params=_params(("parallel", "parallel")),
        name="mlstm",
    )(proj3, proj3, proj3, proj3, conv_w, conv_b, conv_w, conv_b, gcol3, grow, nw, w_cast)


def _attn_kernel(lq1_ref, lk1_ref, lq2_ref, lk2_ref, q_ref, k_ref, v_ref, nw_ref,
                 wa_ref, wb_ref, wc_ref, o_ref, wa_o, wb_o, wc_o,
                 qs_scr, acc1, acc2, m1, l1, m2, l2, *, lam_init, tq, tk, sub):
    for w_ref, w_o in ((wa_ref, wa_o), (wb_ref, wb_o), (wc_ref, wc_o)):
        w_o[...] = w_ref[...].astype(w_o.dtype)
    qi = pl.program_id(2)
    d = D_QK_DIM
    qs_scr[...] = (q_ref[...].astype(_F32) * ((d ** -0.5) * math.log2(math.e))).astype(_BF16)
    n_lane_blk = tk // LANES
    kv_per_q = tq // tk
    streams = ((acc1, m1, l1), (acc2, m2, l2))
    for acc, m, l in streams:
        acc[...] = jnp.zeros(acc.shape, _F32)
        m[...] = jnp.full(m.shape, -jnp.inf, _F32)
        l[...] = jnp.zeros(l.shape, _F32)

    row_i = lax.broadcasted_iota(jnp.int32, (sub, LANES), 0)
    col_i = lax.broadcasted_iota(jnp.int32, (sub, LANES), 1)

    def step(j, diag):
        r0 = pl.multiple_of(j * tk, tk)
        for idx, (acc, m, l) in enumerate(streams):
            kk = k_ref[pl.ds(r0, tk), idx * d:(idx + 1) * d]
            vv = v_ref[pl.ds(r0, tk), :]
            for r in range(tq // sub):
                if diag is not None and (r + 1) * sub - 1 < diag * tk:
                    continue
                rows = pl.ds(r * sub, sub)
                s = _dot_nt(qs_scr[rows, idx * d:(idx + 1) * d], kk)
                blocks = [s[:, b * LANES:(b + 1) * LANES] for b in range(n_lane_blk)]
                if diag is not None:
                    blocks = [blk if diag * tk + (b + 1) * LANES - 1 <= r * sub else
                              jnp.where(col_i + (diag * tk + b * LANES) <= row_i + r * sub, blk, -jnp.inf)
                              for b, blk in enumerate(blocks)]
                m_cur = blocks[0]
                for blk in blocks[1:]:
                    m_cur = jnp.maximum(m_cur, blk)
                m_prev = m[rows, :]
                m_new = jnp.maximum(m_prev, jnp.max(m_cur, axis=-1, keepdims=True))
                alpha = jnp.exp2(m_prev - m_new)
                ps = [jnp.exp2(blk - m_new) for blk in blocks]
                l_add = ps[0]
                for pb in ps[1:]:
                    l_add = l_add + pb
                l[rows, :] = alpha * l[rows, :] + l_add
                p = jnp.concatenate([pb.astype(_BF16) for pb in ps], axis=1)
                alpha2 = jnp.concatenate([alpha] * (D_V_DIM // LANES), axis=1)
                acc[rows, :] = alpha2 * acc[rows, :] + _dot(p, vv)
                m[rows, :] = m_new

    def body(j, carry):
        step(j, None)
        return carry

    lax.fori_loop(0, qi * kv_per_q, body, 0)
    for t in range(kv_per_q):
        step(qi * kv_per_q + t, t)

    lam = (jnp.exp(jnp.sum(lq1_ref[...] * lk1_ref[...], axis=-1, keepdims=True))
           - jnp.exp(jnp.sum(lq2_ref[...] * lk2_ref[...], axis=-1, keepdims=True)) + lam_init)
    o1 = acc1[...] / jnp.sum(l1[...], axis=-1, keepdims=True)
    o2 = acc2[...] / jnp.sum(l2[...], axis=-1, keepdims=True)
    hd = o1 - lam * o2
    y = _rmsnorm_rows(hd, nw_ref[...]) * (1.0 - lam_init)
    o_ref[...] = y.astype(o_ref.dtype)


def _attn(proj3, lq1, lk1, lq2, lk2, nw, lam_init, w_cast, layer, tq=2048, tk=512, sub=128):
    B, S, _ = proj3.shape
    w2 = 2 * D_QK_DIM
    n_q = S // tq
    steps = B * D_HEADS * n_q
    slab = D_MODEL // steps
    lam_spec = pl.BlockSpec((1, D_QK_DIM), lambda b, h, i: (0, 0))
    step_id = lambda b, h, i: (b * D_HEADS + h) * n_q + i
    w_in_spec = pl.BlockSpec((None, slab, D_MODEL), lambda b, h, i: (layer, step_id(b, h, i), 0))
    w_out_spec = pl.BlockSpec((slab, D_MODEL), lambda b, h, i: (step_id(b, h, i), 0))
    w_out_shape = jax.ShapeDtypeStruct((D_MODEL, D_MODEL), _BF16)
    return pl.pallas_call(
        functools.partial(_attn_kernel, lam_init=lam_init, tq=tq, tk=tk, sub=sub),
        grid=(B, D_HEADS, n_q),
        in_specs=[
            lam_spec, lam_spec, lam_spec, lam_spec,
            pl.BlockSpec((None, tq, w2), lambda b, h, i: (b, i, OFF_DQ // w2 + h)),
            pl.BlockSpec((None, S, w2), lambda b, h, i: (b, 0, OFF_DK // w2 + h)),
            pl.BlockSpec((None, S, D_V_DIM), lambda b, h, i: (b, 0, OFF_DV // D_V_DIM + h)),
            pl.BlockSpec((1, D_V_DIM), lambda b, h, i: (0, h)),
            w_in_spec, w_in_spec, w_in_spec,
        ],
        out_specs=[pl.BlockSpec((None, tq, D_V_DIM), lambda b, h, i: (b, i, h)),
                   w_out_spec, w_out_spec, w_out_spec],
        out_shape=[jax.ShapeDtypeStruct((B, S, D_V), _BF16), w_out_shape, w_out_shape, w_out_shape],
        scratch_shapes=[
            pltpu.VMEM((tq, w2), _BF16),
            pltpu.VMEM((tq, D_V_DIM), _F32),
            pltpu.VMEM((tq, D_V_DIM), _F32),
            pltpu.VMEM((tq, LANES), _F32),
            pltpu.VMEM((tq, LANES), _F32),
            pltpu.VMEM((tq, LANES), _F32),
            pltpu.VMEM((tq, LANES), _F32),
        ],
        compiler_params=_params(("parallel", "parallel", "arbitrary")),
        name="diffattn",
    )(lq1, lk1, lq2, lk2, proj3, proj3, proj3, nw, *w_cast)


def _merge_kernel(hm_ref, hd_ref, wm_ref, wd_ref, gm_ref, gd_ref, y_ref):
    a = _dot(hm_ref[...], wm_ref[...])
    b = _dot(hd_ref[...], wd_ref[...])
    y = _sigmoid(gm_ref[...].astype(_F32)) * a + _sigmoid(gd_ref[...].astype(_F32)) * b
    y_ref[...] = y.astype(y_ref.dtype)


def _merge(hm, hd, wm, wd, proj, tm=1024, tn=1024):
    T = hm.shape[0]
    return pl.pallas_call(
        _merge_kernel,
        grid=(T // tm, D_MODEL // tn),
        in_specs=[
            pl.BlockSpec((tm, M_V), lambda i, j: (i, 0)),
            pl.BlockSpec((tm, D_V), lambda i, j: (i, 0)),
            pl.BlockSpec((M_V, tn), lambda i, j: (0, j)),
            pl.BlockSpec((D_V, tn), lambda i, j: (0, j)),
            pl.BlockSpec((tm, tn), lambda i, j: (i, OFF_GM // tn + j)),
            pl.BlockSpec((tm, tn), lambda i, j: (i, OFF_GD // tn + j)),
        ],
        out_specs=pl.BlockSpec((tm, tn), lambda i, j: (i, j)),
        out_shape=jax.ShapeDtypeStruct((T, D_MODEL), _BF16),
        compiler_params=_params(("parallel", "parallel")),
        name="merge",
    )(hm, hd, wm, wd, proj, proj)


def _resmm_kernel(a_ref, w_ref, x_ref, o_ref):
    o_ref[...] = x_ref[...] + _dot(a_ref[...], w_ref[...])


def _resmm(a, w, x, tm, tn, name):
    T, K = a.shape
    N = w.shape[1]
    return pl.pallas_call(
        _resmm_kernel,
        grid=(T // tm, N // tn),
        in_specs=[
            pl.BlockSpec((tm, K), lambda i, j: (i, 0)),
            pl.BlockSpec((K, tn), lambda i, j: (0, j)),
            pl.BlockSpec((tm, tn), lambda i, j: (i, j)),
        ],
        out_specs=pl.BlockSpec((tm, tn), lambda i, j: (i, j)),
        out_shape=jax.ShapeDtypeStruct((T, N), _F32),
        compiler_params=_params(("parallel", "parallel")),
        name=name,
    )(a, w, x)


def _ffn_up_kernel(x_ref, nw_ref, wg_ref, wu_ref, a_ref, h_scr):
    @pl.when(pl.program_id(1) == 0)
    def _():
        h_scr[...] = _rmsnorm_rows(x_ref[...], nw_ref[...]).astype(_BF16)

    h = h_scr[...]
    g = _dot(h, wg_ref[...].astype(_BF16))
    u = _dot(h, wu_ref[...].astype(_BF16))
    a_ref[...] = (g * _sigmoid(g) * u).astype(a_ref.dtype)


def _ffn_up(x, nw, wg, wu, layer, tm=1024, tn=512):
    T = x.shape[0]
    return pl.pallas_call(
        _ffn_up_kernel,
        grid=(T // tm, D_FF // tn),
        in_specs=[
            pl.BlockSpec((tm, D_MODEL), lambda i, j: (i, 0)),
            pl.BlockSpec((1, D_MODEL), lambda i, j: (0, 0)),
            pl.BlockSpec((None, D_MODEL, tn), lambda i, j: (layer, 0, j)),
            pl.BlockSpec((None, D_MODEL, tn), lambda i, j: (layer, 0, j)),
        ],
        out_specs=pl.BlockSpec((tm, tn), lambda i, j: (i, j)),
        out_shape=jax.ShapeDtypeStruct((T, D_FF), _BF16),
        scratch_shapes=[pltpu.VMEM((tm, D_MODEL), _BF16)],
        compiler_params=_params(("parallel", "arbitrary")),
        name="ffn_up",
    )(x, nw, wg, wu)


def _norm_kernel(x_ref, nw_ref, o_ref):
    o_ref[...] = _rmsnorm_rows(x_ref[...], nw_ref[...])


def _final_norm(x, nw, tm=512):
    T = x.shape[0]
    return pl.pallas_call(
        _norm_kernel,
        grid=(T // tm,),
        in_specs=[
            pl.BlockSpec((tm, D_MODEL), lambda i: (i, 0)),
            pl.BlockSpec((1, D_MODEL), lambda i: (0, 0)),
        ],
        out_specs=pl.BlockSpec((tm, D_MODEL), lambda i: (i, 0)),
        out_shape=jax.ShapeDtypeStruct((T, D_MODEL), _F32),
        compiler_params=_params(("parallel",)),
        name="final_norm",
    )(x, nw)


def kernel(x, attn_norm_w, w_in, conv_w, conv_b, b_igate, b_fgate, mlstm_norm_w, lambda_q1, lambda_k1, lambda_q2, lambda_k2, diff_norm_w, w_branch_m, w_branch_d, w_out, ffn_norm_w, w_ffn_gate, w_ffn_up, w_ffn_down, final_norm_w):
    B, S, D = x.shape
    T = B * S
    xf = x.reshape(T, D)
    gate_pad = jnp.zeros((LANES - 2 * M_HEADS,), _F32)
    w_in_t = jnp.swapaxes(w_in, 1, 2)
    for l in range(DEPTH):
        lam_init = 0.8 - 0.6 * math.exp(-0.3 * l)
        gate_bias = jnp.concatenate([b_igate[l], b_fgate[l], gate_pad]).reshape(1, LANES)

        proj, gpre = _inproj(xf, attn_norm_w[l].reshape(1, D), w_in_t, l)
        gcol, grow = _gates(gpre, gate_bias)
        proj3 = proj.reshape(B, S, N_PROJ)
        hm, w_fd = _mlstm(proj3, conv_w[l], conv_b[l].reshape(1, 2 * M_QK), gcol.reshape(B, S, LANES), grow,
                          mlstm_norm_w[l].reshape(1, M_V), w_ffn_down, l)
        hd, w_bm, w_bd, w_o = _attn(proj3, lambda_q1[l].reshape(1, D_QK_DIM), lambda_k1[l].reshape(1, D_QK_DIM),
                                    lambda_q2[l].reshape(1, D_QK_DIM), lambda_k2[l].reshape(1, D_QK_DIM),
                                    diff_norm_w[l].reshape(1, D_V), lam_init,
                                    (w_branch_m, w_branch_d, w_out), l)
        y = _merge(hm.reshape(T, M_V), hd.reshape(T, D_V), w_bm, w_bd, proj)
        xf = _resmm(y, w_o, xf, 1024, 1024, "out_proj")
        a = _ffn_up(xf, ffn_norm_w[l].reshape(1, D), w_ffn_gate, w_ffn_up, l)
        xf = _resmm(a, w_fd, xf, 1024, 512, "ffn_down")
    out = _final_norm(xf, final_norm_w.reshape(1, D))
    return out.reshape(B, S, D)
```

```python
import functools
import math

import jax
import jax.numpy as jnp
from jax import lax
from jax.experimental import pallas as pl
from jax.experimental.pallas import tpu as pltpu

D_MODEL = 2048
DEPTH = 4
M_HEADS = 8
M_QK_DIM = 128
M_V_DIM = 256
CONV_K = 4
M_QK = M_HEADS * M_QK_DIM
M_V = M_HEADS * M_V_DIM
D_HEADS = 8
D_QK_DIM = 128
D_V_DIM = 256
D_Q = D_HEADS * 2 * D_QK_DIM
D_V = D_HEADS * D_V_DIM
D_FF = 5632
EPS = 1e-6

IN_GATE_LO = 2 * M_QK + 2 * M_V
IN_GATE_HI = IN_GATE_LO + 2 * M_HEADS
OFF_MQ = 0
OFF_MK = M_QK
OFF_MV = 2 * M_QK
OFF_MO = OFF_MV + M_V
OFF_DQ = OFF_MO + M_V
OFF_DK = OFF_DQ + D_Q
OFF_DV = OFF_DK + D_Q
OFF_GM = OFF_DV + D_V
OFF_GD = OFF_GM + D_MODEL
N_PROJ = OFF_GD + D_MODEL

LANES = 128
SUBLANES = 8
M_CHUNK = 256
GATE_ROWS = 1024
VMEM_LIMIT = 56 * 1024 * 1024

_F32 = jnp.float32
_BF16 = jnp.bfloat16


def _dot(a, b):
    return jnp.dot(a, b, preferred_element_type=_F32)


def _dot_nt(a, b):
    return lax.dot_general(a, b, (((1,), (1,)), ((), ())), preferred_element_type=_F32)


def _dot_tn(a, b):
    return lax.dot_general(a, b, (((0,), (0,)), ((), ())), preferred_element_type=_F32)


def _sigmoid(x):
    return 1.0 / (1.0 + jnp.exp(-x))


def _rmsnorm_rows(x, w):
    ms = jnp.mean(x * x, axis=-1, keepdims=True)
    return x * lax.rsqrt(ms + EPS) * w


def _params(sem):
    return pltpu.CompilerParams(dimension_semantics=sem, vmem_limit_bytes=VMEM_LIMIT)


def _inproj_kernel(x_ref, nw_ref, w_ref, wg_ref, o_ref, g_ref, h_scr):
    @pl.when(pl.program_id(1) == 0)
    def _():
        hb = _rmsnorm_rows(x_ref[...], nw_ref[...]).astype(_BF16)
        h_scr[...] = hb
        g_ref[...] = _dot_nt(hb, wg_ref[0].astype(_BF16))

    w = w_ref[0] if len(w_ref.shape) == 3 else w_ref[...]
    o_ref[...] = _dot_nt(h_scr[...], w.astype(_BF16)).astype(o_ref.dtype)


def _proj_rows(blk, rows):
    n_lo = IN_GATE_LO // rows
    start = jnp.where(blk < n_lo, blk * rows, IN_GATE_HI + (blk - n_lo) * rows)
    return pl.multiple_of(start, 2 * M_HEADS)


def _inproj(x, nw, w_in_t, layer, w_proj=None, tm=1024, tn=1024):
    T = x.shape[0]
    el = pl.Element
    if w_proj is None:
        w_arg = w_in_t
        w_spec = pl.BlockSpec((el(1), el(tn), el(D_MODEL)), lambda i, j: (layer, _proj_rows(j, tn), 0))
    else:
        w_arg = w_proj
        w_spec = pl.BlockSpec((tn, D_MODEL), lambda i, j: (j, 0))
    return pl.pallas_call(
        _inproj_kernel,
        grid=(T // tm, N_PROJ // tn),
        in_specs=[
            pl.BlockSpec((tm, D_MODEL), lambda i, j: (i, 0)),
            pl.BlockSpec((1, D_MODEL), lambda i, j: (0, 0)),
            w_spec,
            pl.BlockSpec((el(1), el(LANES), el(D_MODEL)), lambda i, j: (layer, IN_GATE_LO, 0)),
        ],
        out_specs=[
            pl.BlockSpec((tm, tn), lambda i, j: (i, j)),
            pl.BlockSpec((tm, LANES), lambda i, j: (i, 0)),
        ],
        out_shape=[
            jax.ShapeDtypeStruct((T, N_PROJ), _BF16),
            jax.ShapeDtypeStruct((T, LANES), _F32),
        ],
        scratch_shapes=[pltpu.VMEM((tm, D_MODEL), _BF16)],
        compiler_params=_params(("parallel", "arbitrary")),
        name="inproj",
    )(x, nw, w_arg, w_in_t)


def _gates_kernel(g_ref, b_ref, gcol_ref, grow_ref):
    z = g_ref[...] + b_ref[...]
    rows = z.shape[0]
    lane = lax.broadcasted_iota(jnp.int32, z.shape, 1)
    lf = jnp.minimum(z, 0.0) - jnp.log1p(jnp.exp(-jnp.abs(z)))
    pos = lax.broadcasted_iota(jnp.int32, z.shape, 0) % M_CHUNK
    csum = lf
    d = 1
    while d < M_CHUNK:
        csum = csum + jnp.where(pos >= d, pltpu.roll(csum, d, 0), 0.0)
        d *= 2
    is_f = (lane >= M_HEADS) & (lane < 2 * M_HEADS)
    col = jnp.where(is_f, csum, z)
    gcol_ref[...] = col
    tr = col.T
    grow_ref[...] = tr[0:M_HEADS, :] - tr[M_HEADS:2 * M_HEADS, :]


def _gates(gpre, bias):
    T = gpre.shape[0]
    return pl.pallas_call(
        _gates_kernel,
        grid=(T // GATE_ROWS,),
        in_specs=[
            pl.BlockSpec((GATE_ROWS, LANES), lambda i: (i, 0)),
            pl.BlockSpec((1, LANES), lambda i: (0, 0)),
        ],
        out_specs=[
            pl.BlockSpec((GATE_ROWS, LANES), lambda i: (i, 0)),
            pl.BlockSpec((M_HEADS, GATE_ROWS), lambda i: (0, i)),
        ],
        out_shape=[
            jax.ShapeDtypeStruct((T, LANES), _F32),
            jax.ShapeDtypeStruct((M_HEADS, T), _F32),
        ],
        compiler_params=_params(("parallel",)),
        name="gates",
    )(gpre, bias)


def _mlstm_kernel(q_ref, k_ref, v_ref, o_ref, cwq_ref, cbq_ref, cwk_ref, cbk_ref,
                  gcol_ref, grow_ref, nw_ref, *rest, n_side):
    side_in, out_ref, side_out = rest[:n_side], rest[n_side], rest[n_side + 1:2 * n_side + 1]
    xq_scr, xk_scr, c_scr, m_scr = rest[2 * n_side + 1:]
    for w_ref, w_o in zip(side_in, side_out):
        w_o[...] = w_ref[...].astype(w_o.dtype)
    head = pl.program_id(1)
    S = q_ref.shape[0]
    L = M_CHUNK
    pad = SUBLANES
    zpad = jnp.zeros((pad, M_QK_DIM), _F32)
    xq_scr[0:pad, :] = zpad
    xk_scr[0:pad, :] = zpad
    xq_scr[pad:, :] = q_ref[...].astype(_F32)
    xk_scr[pad:, :] = k_ref[...].astype(_F32)
    c_scr[...] = jnp.zeros(c_scr.shape, _F32)
    m_scr[...] = jnp.zeros(m_scr.shape, _F32)

    lane = lax.broadcasted_iota(jnp.int32, (L, LANES), 1)
    row_i = lax.broadcasted_iota(jnp.int32, (L, L), 0)
    col_i = lax.broadcasted_iota(jnp.int32, (L, L), 1)
    tril = col_i <= row_i
    ones_aug = jnp.ones((L, LANES), _BF16)
    k_scale = M_QK_DIM ** -0.5

    def conv_silu(x_scr, cw_ref, cb_ref, r0):
        acc = cb_ref[...] + cw_ref[CONV_K - 1:CONV_K, :] * x_scr[pl.ds(r0 + pad, L), :]
        for j in range(CONV_K - 1):
            acc = acc + cw_ref[j:j + 1, :] * x_scr[pl.ds(r0 + pad - (CONV_K - 1 - j), L), :]
        return acc * _sigmoid(acc)

    def chunk(c, carry):
        r0 = pl.multiple_of(c * L, L)
        qf = conv_silu(xq_scr, cwq_ref, cbq_ref, r0)
        kf = conv_silu(xk_scr, cwk_ref, cbk_ref, r0) * k_scale
        qb = qf.astype(_BF16)
        kb = kf.astype(_BF16)
        vaug = jnp.concatenate([v_ref[pl.ds(r0, L), :], ones_aug], axis=1)

        g = gcol_ref[pl.ds(r0, L), :]
        ig = jnp.sum(jnp.where(lane == head, g, 0.0), axis=-1, keepdims=True)
        b = jnp.sum(jnp.where(lane == head + M_HEADS, g, 0.0), axis=-1, keepdims=True)
        r = grow_ref[pl.ds(head, 1), pl.ds(r0, L)]
        m_prev = m_scr[...]

        log_d = jnp.where(tril, b + r, -jnp.inf)
        log_inter = b + m_prev
        m_row = jnp.maximum(log_inter, jnp.max(log_d, axis=-1, keepdims=True))
        dmat = jnp.exp(log_d - m_row)
        inter = jnp.exp(log_inter - m_row)
        s = _dot_nt(qb, kb) * dmat
        num_aug = _dot(s.astype(_BF16), vaug) + inter * _dot(qb, c_scr[...].astype(_BF16))
        num = num_aug[:, :M_V_DIM]
        den = num_aug[:, M_V_DIM:M_V_DIM + 1]
        hh = num / jnp.maximum(jnp.abs(den), jnp.exp(-m_row))
        y = _rmsnorm_rows(hh, nw_ref[...])
        gate = _sigmoid(o_ref[pl.ds(r0, L), :].astype(_F32))
        out_ref[pl.ds(r0, L), :] = (gate * y).astype(out_ref.dtype)

        b_last = b[L - 1:L, :]
        log_w = b_last - b + ig
        m_new = jnp.maximum(b_last + m_prev, jnp.max(log_w, axis=0, keepdims=True))
        w = jnp.exp(log_w - m_new)
        decay = jnp.exp(b_last + m_prev - m_new)
        kw = (kf * w).astype(_BF16)
        c_scr[...] = decay * c_scr[...] + _dot_tn(kw, vaug)
        m_scr[...] = m_new
        return carry

    lax.fori_loop(0, S // L, chunk, 0)


def _mlstm(proj3, conv_w, conv_b, gcol3, grow, nw, w_cast, layer):
    B, S, _ = proj3.shape
    steps = B * M_HEADS
    side_in, side_out, side_shape = [], [], []
    for w in w_cast:
        _, wk, wn = w.shape
        slab = wk // steps
        side_in.append(pl.BlockSpec((None, slab, wn), lambda b, h: (layer, b * M_HEADS + h, 0)))
        side_out.append(pl.BlockSpec((slab, wn), lambda b, h: (b * M_HEADS + h, 0)))
        side_shape.append(jax.ShapeDtypeStruct((wk, wn), _BF16))
    qk_blk = lambda off: (lambda b, h: (b, 0, off // M_QK_DIM + h))
    v_blk = lambda off: (lambda b, h: (b, 0, off // M_V_DIM + h))
    return pl.pallas_call(
        functools.partial(_mlstm_kernel, n_side=len(w_cast)),
        grid=(B, M_HEADS),
        in_specs=[
            pl.BlockSpec((None, S, M_QK_DIM), qk_blk(OFF_MQ)),
            pl.BlockSpec((None, S, M_QK_DIM), qk_blk(OFF_MK)),
            pl.BlockSpec((None, S, M_V_DIM), v_blk(OFF_MV)),
            pl.BlockSpec((None, S, M_V_DIM), v_blk(OFF_MO)),
            pl.BlockSpec((CONV_K, M_QK_DIM), lambda b, h: (0, h)),
            pl.BlockSpec((1, M_QK_DIM), lambda b, h: (0, h)),
            pl.BlockSpec((CONV_K, M_QK_DIM), lambda b, h: (0, M_HEADS + h)),
            pl.BlockSpec((1, M_QK_DIM), lambda b, h: (0, M_HEADS + h)),
            pl.BlockSpec((None, S, LANES), lambda b, h: (b, 0, 0)),
            pl.BlockSpec((M_HEADS, S), lambda b, h: (0, b)),
            pl.BlockSpec((1, M_V_DIM), lambda b, h: (0, h)),
        ] + side_in,
        out_specs=[pl.BlockSpec((None, S, M_V_DIM), lambda b, h: (b, 0, h))] + side_out,
        out_shape=[jax.ShapeDtypeStruct((B, S, M_V), _BF16)] + side_shape,
        scratch_shapes=[
            pltpu.VMEM((S + SUBLANES, M_QK_DIM), _F32),
            pltpu.VMEM((S + SUBLANES, M_QK_DIM), _F32),
            pltpu.VMEM((M_QK_DIM, M_V_DIM + LANES), _F32),
            pltpu.VMEM((1, 1), _F32),
        ],
        compiler_params=_params(("parallel", "parallel")),
        name="mlstm",
    )(proj3, proj3, proj3, proj3, conv_w, conv_b, conv_w, conv_b, gcol3, grow, nw, *w_cast)


def _attn_kernel(lq1_ref, lk1_ref, lq2_ref, lk2_ref, q_ref, k_ref, v_ref, nw_ref,
                 *rest, n_side, lam_init, tq, tk, sub):
    side_in, o_ref, side_out = rest[:n_side], rest[n_side], rest[n_side + 1:2 * n_side + 1]
    qs_scr, acc1, acc2, m1, l1, m2, l2 = rest[2 * n_side + 1:]
    for w_ref, w_o in zip(side_in, side_out):
        w_o[...] = (w_ref[0] if len(w_ref.shape) == 3 else w_ref[...]).astype(w_o.dtype)
    qi = pl.program_id(2)
    d = D_QK_DIM
    qs_scr[...] = (q_ref[...].astype(_F32) * ((d ** -0.5) * math.log2(math.e))).astype(_BF16)
    n_lane_blk = tk // LANES
    kv_per_q = tq // tk
    streams = ((acc1, m1, l1), (acc2, m2, l2))
    for acc, m, l in streams:
        acc[...] = jnp.zeros(acc.shape, _F32)
        m[...] = jnp.full(m.shape, -jnp.inf, _F32)
        l[...] = jnp.zeros(l.shape, _F32)

    row_i = lax.broadcasted_iota(jnp.int32, (sub, LANES), 0)
    col_i = lax.broadcasted_iota(jnp.int32, (sub, LANES), 1)

    def step(j, diag):
        r0 = pl.multiple_of(j * tk, tk)
        for idx, (acc, m, l) in enumerate(streams):
            kk = k_ref[pl.ds(r0, tk), idx * d:(idx + 1) * d]
            vv = v_ref[pl.ds(r0, tk), :]
            for r in range(tq // sub):
                if diag is not None and (r + 1) * sub - 1 < diag * tk:
                    continue
                rows = pl.ds(r * sub, sub)
                s = _dot_nt(qs_scr[rows, idx * d:(idx + 1) * d], kk)
                blocks = [s[:, b * LANES:(b + 1) * LANES] for b in range(n_lane_blk)]
                if diag is not None:
                    blocks = [blk if diag * tk + (b + 1) * LANES - 1 <= r * sub else
                              jnp.where(col_i + (diag * tk + b * LANES) <= row_i + r * sub, blk, -jnp.inf)
                              for b, blk in enumerate(blocks)]
                m_cur = blocks[0]
                for blk in blocks[1:]:
                    m_cur = jnp.maximum(m_cur, blk)
                m_prev = m[rows, :]
                m_new = jnp.maximum(m_prev, jnp.max(m_cur, axis=-1, keepdims=True))
                alpha = jnp.exp2(m_prev - m_new)
                ps = [jnp.exp2(blk - m_new) for blk in blocks]
                l_add = ps[0]
                for pb in ps[1:]:
                    l_add = l_add + pb
                l[rows, :] = alpha * l[rows, :] + l_add
                p = jnp.concatenate([pb.astype(_BF16) for pb in ps], axis=1)
                alpha2 = jnp.concatenate([alpha] * (D_V_DIM // LANES), axis=1)
                acc[rows, :] = alpha2 * acc[rows, :] + _dot(p, vv)
                m[rows, :] = m_new

    def body(j, carry):
        step(j, None)
        return carry

    lax.fori_loop(0, qi * kv_per_q, body, 0)
    for t in range(kv_per_q):
        step(qi * kv_per_q + t, t)

    lam = (jnp.exp(jnp.sum(lq1_ref[...] * lk1_ref[...], axis=-1, keepdims=True))
           - jnp.exp(jnp.sum(lq2_ref[...] * lk2_ref[...], axis=-1, keepdims=True)) + lam_init)
    o1 = acc1[...] / jnp.sum(l1[...], axis=-1, keepdims=True)
    o2 = acc2[...] / jnp.sum(l2[...], axis=-1, keepdims=True)
    hd = o1 - lam * o2
    y = _rmsnorm_rows(hd, nw_ref[...]) * (1.0 - lam_init)
    o_ref[...] = y.astype(o_ref.dtype)


def _attn(proj3, lq1, lk1, lq2, lk2, nw, lam_init, w_cast, layer, w_in_t=None, tq=2048, tk=512, sub=128):
    B, S, _ = proj3.shape
    w2 = 2 * D_QK_DIM
    n_q = S // tq
    steps = B * D_HEADS * n_q
    slab = D_MODEL // steps
    lam_spec = pl.BlockSpec((1, D_QK_DIM), lambda b, h, i: (0, 0))
    step_id = lambda b, h, i: (b * D_HEADS + h) * n_q + i
    w_in_spec = pl.BlockSpec((None, slab, D_MODEL), lambda b, h, i: (layer, step_id(b, h, i), 0))
    w_out_spec = pl.BlockSpec((slab, D_MODEL), lambda b, h, i: (step_id(b, h, i), 0))
    w_out_shape = jax.ShapeDtypeStruct((D_MODEL, D_MODEL), _BF16)
    side_in = [w_in_spec] * len(w_cast)
    side_out = [w_out_spec] * len(w_cast)
    side_shape = [w_out_shape] * len(w_cast)
    side_args = list(w_cast)
    if w_in_t is not None:
        rows = N_PROJ // steps
        el = pl.Element
        side_in.append(pl.BlockSpec((el(1), el(rows), el(D_MODEL)),
                                    lambda b, h, i: (layer + 1, _proj_rows(step_id(b, h, i), rows), 0)))
        side_out.append(pl.BlockSpec((rows, D_MODEL), lambda b, h, i: (step_id(b, h, i), 0)))
        side_shape.append(jax.ShapeDtypeStruct((N_PROJ, D_MODEL), _BF16))
        side_args.append(w_in_t)
    return pl.pallas_call(
        functools.partial(_attn_kernel, n_side=len(side_args), lam_init=lam_init, tq=tq, tk=tk, sub=sub),
        grid=(B, D_HEADS, n_q),
        in_specs=[
            lam_spec, lam_spec, lam_spec, lam_spec,
            pl.BlockSpec((None, tq, w2), lambda b, h, i: (b, i, OFF_DQ // w2 + h)),
            pl.BlockSpec((None, S, w2), lambda b, h, i: (b, 0, OFF_DK // w2 + h)),
            pl.BlockSpec((None, S, D_V_DIM), lambda b, h, i: (b, 0, OFF_DV // D_V_DIM + h)),
            pl.BlockSpec((1, D_V_DIM), lambda b, h, i: (0, h)),
        ] + side_in,
        out_specs=[pl.BlockSpec((None, tq, D_V_DIM), lambda b, h, i: (b, i, h))] + side_out,
        out_shape=[jax.ShapeDtypeStruct((B, S, D_V), _BF16)] + side_shape,
        scratch_shapes=[
            pltpu.VMEM((tq, w2), _BF16),
            pltpu.VMEM((tq, D_V_DIM), _F32),
            pltpu.VMEM((tq, D_V_DIM), _F32),
            pltpu.VMEM((tq, LANES), _F32),
            pltpu.VMEM((tq, LANES), _F32),
            pltpu.VMEM((tq, LANES), _F32),
            pltpu.VMEM((tq, LANES), _F32),
        ],
        compiler_params=_params(("parallel", "parallel", "arbitrary")),
        name="diffattn",
    )(lq1, lk1, lq2, lk2, proj3, proj3, proj3, nw, *side_args)


def _merge_kernel(hm_ref, hd_ref, wm_ref, wd_ref, gm_ref, gd_ref, y_ref):
    a = _dot(hm_ref[...], wm_ref[...])
    b = _dot(hd_ref[...], wd_ref[...])
    y = _sigmoid(gm_ref[...].astype(_F32)) * a + _sigmoid(gd_ref[...].astype(_F32)) * b
    y_ref[...] = y.astype(y_ref.dtype)


def _merge(hm, hd, wm, wd, proj, tm=1024, tn=1024):
    T = hm.shape[0]
    return pl.pallas_call(
        _merge_kernel,
        grid=(T // tm, D_MODEL // tn),
        in_specs=[
            pl.BlockSpec((tm, M_V), lambda i, j: (i, 0)),
            pl.BlockSpec((tm, D_V), lambda i, j: (i, 0)),
            pl.BlockSpec((M_V, tn), lambda i, j: (0, j)),
            pl.BlockSpec((D_V, tn), lambda i, j: (0, j)),
            pl.BlockSpec((tm, tn), lambda i, j: (i, OFF_GM // tn + j)),
            pl.BlockSpec((tm, tn), lambda i, j: (i, OFF_GD // tn + j)),
        ],
        out_specs=pl.BlockSpec((tm, tn), lambda i, j: (i, j)),
        out_shape=jax.ShapeDtypeStruct((T, D_MODEL), _BF16),
        compiler_params=_params(("parallel", "parallel")),
        name="merge",
    )(hm, hd, wm, wd, proj, proj)


def _resmm_kernel(a_ref, w_ref, x_ref, o_ref):
    o_ref[...] = x_ref[...] + _dot(a_ref[...], w_ref[...])


def _resmm(a, w, x, tm, tn, name):
    T, K = a.shape
    N = w.shape[1]
    return pl.pallas_call(
        _resmm_kernel,
        grid=(T // tm, N // tn),
        in_specs=[
            pl.BlockSpec((tm, K), lambda i, j: (i, 0)),
            pl.BlockSpec((K, tn), lambda i, j: (0, j)),
            pl.BlockSpec((tm, tn), lambda i, j: (i, j)),
        ],
        out_specs=pl.BlockSpec((tm, tn), lambda i, j: (i, j)),
        out_shape=jax.ShapeDtypeStruct((T, N), _F32),
        compiler_params=_params(("parallel", "parallel")),
        name=name,
    )(a, w, x)


def _ffn_up_kernel(x_ref, nw_ref, wg_ref, wu_ref, a_ref, h_scr):
    @pl.when(pl.program_id(1) == 0)
    def _():
        h_scr[...] = _rmsnorm_rows(x_ref[...], nw_ref[...]).astype(_BF16)

    h = h_scr[...]
    g = _dot(h, wg_ref[...])
    u = _dot(h, wu_ref[...])
    a_ref[...] = (g * _sigmoid(g) * u).astype(a_ref.dtype)


def _ffn_up(x, nw, wg, wu, tm=1024, tn=512):
    T = x.shape[0]
    return pl.pallas_call(
        _ffn_up_kernel,
        grid=(T // tm, D_FF // tn),
        in_specs=[
            pl.BlockSpec((tm, D_MODEL), lambda i, j: (i, 0)),
            pl.BlockSpec((1, D_MODEL), lambda i, j: (0, 0)),
            pl.BlockSpec((D_MODEL, tn), lambda i, j: (0, j)),
            pl.BlockSpec((D_MODEL, tn), lambda i, j: (0, j)),
        ],
        out_specs=pl.BlockSpec((tm, tn), lambda i, j: (i, j)),
        out_shape=jax.ShapeDtypeStruct((T, D_FF), _BF16),
        scratch_shapes=[pltpu.VMEM((tm, D_MODEL), _BF16)],
        compiler_params=_params(("parallel", "arbitrary")),
        name="ffn_up",
    )(x, nw, wg, wu)


def _norm_kernel(x_ref, nw_ref, o_ref):
    o_ref[...] = _rmsnorm_rows(x_ref[...], nw_ref[...])


def _final_norm(x, nw, tm=512):
    T = x.shape[0]
    return pl.pallas_call(
        _norm_kernel,
        grid=(T // tm,),
        in_specs=[
            pl.BlockSpec((tm, D_MODEL), lambda i: (i, 0)),
            pl.BlockSpec((1, D_MODEL), lambda i: (0, 0)),
        ],
        out_specs=pl.BlockSpec((tm, D_MODEL), lambda i: (i, 0)),
        out_shape=jax.ShapeDtypeStruct((T, D_MODEL), _F32),
        compiler_params=_params(("parallel",)),
        name="final_norm",
    )(x, nw)


def kernel(x, attn_norm_w, w_in, conv_w, conv_b, b_igate, b_fgate, mlstm_norm_w, lambda_q1, lambda_k1, lambda_q2, lambda_k2, diff_norm_w, w_branch_m, w_branch_d, w_out, ffn_norm_w, w_ffn_gate, w_ffn_up, w_ffn_down, final_norm_w):
    B, S, D = x.shape
    T = B * S
    xf = x.reshape(T, D)
    gate_pad = jnp.zeros((LANES - 2 * M_HEADS,), _F32)
    w_in_t = jnp.swapaxes(w_in, 1, 2)
    w_proj = None
    for l in range(DEPTH):
        lam_init = 0.8 - 0.6 * math.exp(-0.3 * l)
        gate_bias = jnp.concatenate([b_igate[l], b_fgate[l], gate_pad]).reshape(1, LANES)

        proj, gpre = _inproj(xf, attn_norm_w[l].reshape(1, D), w_in_t, l, w_proj)
        gcol, grow = _gates(gpre, gate_bias)
        proj3 = proj.reshape(B, S, N_PROJ)
        hm, w_fd, w_fg, w_fu = _mlstm(proj3, conv_w[l], conv_b[l].reshape(1, 2 * M_QK),
                                      gcol.reshape(B, S, LANES), grow, mlstm_norm_w[l].reshape(1, M_V),
                                      (w_ffn_down, w_ffn_gate, w_ffn_up), l)
        attn_out = _attn(proj3, lambda_q1[l].reshape(1, D_QK_DIM), lambda_k1[l].reshape(1, D_QK_DIM),
                         lambda_q2[l].reshape(1, D_QK_DIM), lambda_k2[l].reshape(1, D_QK_DIM),
                         diff_norm_w[l].reshape(1, D_V), lam_init, (w_branch_m, w_branch_d, w_out), l,
                         w_in_t if l + 1 < DEPTH else None)
        hd, w_bm, w_bd, w_o = attn_out[:4]
        w_proj = attn_out[4] if l + 1 < DEPTH else None
        y = _merge(hm.reshape(T, M_V), hd.reshape(T, D_V), w_bm, w_bd, proj)
        xf = _resmm(y, w_o, xf, 1024, 1024, "out_proj")
        a = _ffn_up(xf, ffn_norm_w[l].reshape(1, D), w_fg, w_fu)
        xf = _resmm(a, w_fd, xf, 1024, 512, "ffn_down")
    out = _final_norm(xf, final_norm_w.reshape(1, D))
    return out.reshape(B, S, D)
```

```python
import functools
import math

import jax
import jax.numpy as jnp
from jax import lax
from jax.experimental import pallas as pl
from jax.experimental.pallas import tpu as pltpu

D_MODEL = 2048
DEPTH = 4
M_HEADS = 8
M_QK_DIM = 128
M_V_DIM = 256
CONV_K = 4
M_QK = M_HEADS * M_QK_DIM
M_V = M_HEADS * M_V_DIM
D_HEADS = 8
D_QK_DIM = 128
D_V_DIM = 256
D_Q = D_HEADS * 2 * D_QK_DIM
D_V = D_HEADS * D_V_DIM
D_FF = 5632
EPS = 1e-6

IN_GATE_LO = 2 * M_QK + 2 * M_V
IN_GATE_HI = IN_GATE_LO + 2 * M_HEADS
OFF_MQ = 0
OFF_MK = M_QK
OFF_MV = 2 * M_QK
OFF_MO = OFF_MV + M_V
OFF_DQ = OFF_MO + M_V
OFF_DK = OFF_DQ + D_Q
OFF_DV = OFF_DK + D_Q
OFF_GM = OFF_DV + D_V
OFF_GD = OFF_GM + D_MODEL
N_PROJ = OFF_GD + D_MODEL

LANES = 128
SUBLANES = 8
M_CHUNK = 256
GATE_ROWS = 1024
VMEM_LIMIT = 56 * 1024 * 1024

_F32 = jnp.float32
_BF16 = jnp.bfloat16


def _dot(a, b):
    return jnp.dot(a, b, preferred_element_type=_F32)


def _dot_nt(a, b):
    return lax.dot_general(a, b, (((1,), (1,)), ((), ())), preferred_element_type=_F32)


def _dot_tn(a, b):
    return lax.dot_general(a, b, (((0,), (0,)), ((), ())), preferred_element_type=_F32)


def _sigmoid(x):
    return 1.0 / (1.0 + jnp.exp(-x))


def _rmsnorm_rows(x, w):
    ms = jnp.mean(x * x, axis=-1, keepdims=True)
    return x * lax.rsqrt(ms + EPS) * w


def _params(sem):
    return pltpu.CompilerParams(dimension_semantics=sem, vmem_limit_bytes=VMEM_LIMIT)


def _inproj_kernel(x_ref, nw_ref, w_ref, wg_ref, o_ref, g_ref, h_scr):
    @pl.when(pl.program_id(1) == 0)
    def _():
        hb = _rmsnorm_rows(x_ref[...], nw_ref[...]).astype(_BF16)
        h_scr[...] = hb
        g_ref[...] = _dot_nt(hb, wg_ref[0].astype(_BF16))

    w = w_ref[0] if len(w_ref.shape) == 3 else w_ref[...]
    o_ref[...] = _dot_nt(h_scr[...], w.astype(_BF16)).astype(o_ref.dtype)


def _proj_rows(blk, rows):
    n_lo = IN_GATE_LO // rows
    start = jnp.where(blk < n_lo, blk * rows, IN_GATE_HI + (blk - n_lo) * rows)
    return pl.multiple_of(start, 2 * M_HEADS)


def _inproj(x, nw, w_in_t, layer, w_proj=None, tm=1024, tn=1024):
    T = x.shape[0]
    el = pl.Element
    if w_proj is None:
        w_arg = w_in_t
        w_spec = pl.BlockSpec((el(1), el(tn), el(D_MODEL)), lambda i, j: (layer, _proj_rows(j, tn), 0))
    else:
        w_arg = w_proj
        w_spec = pl.BlockSpec((tn, D_MODEL), lambda i, j: (j, 0))
    return pl.pallas_call(
        _inproj_kernel,
        grid=(T // tm, N_PROJ // tn),
        in_specs=[
            pl.BlockSpec((tm, D_MODEL), lambda i, j: (i, 0)),
            pl.BlockSpec((1, D_MODEL), lambda i, j: (0, 0)),
            w_spec,
            pl.BlockSpec((el(1), el(LANES), el(D_MODEL)), lambda i, j: (layer, IN_GATE_LO, 0)),
        ],
        out_specs=[
            pl.BlockSpec((tm, tn), lambda i, j: (i, j)),
            pl.BlockSpec((tm, LANES), lambda i, j: (i, 0)),
        ],
        out_shape=[
            jax.ShapeDtypeStruct((T, N_PROJ), _BF16),
            jax.ShapeDtypeStruct((T, LANES), _F32),
        ],
        scratch_shapes=[pltpu.VMEM((tm, D_MODEL), _BF16)],
        compiler_params=_params(("parallel", "arbitrary")),
        name="inproj",
    )(x, nw, w_arg, w_in_t)


def _gates_kernel(g_ref, b_ref, gcol_ref, grow_ref):
    z = g_ref[...] + b_ref[...]
    rows = z.shape[0]
    lane = lax.broadcasted_iota(jnp.int32, z.shape, 1)
    lf = jnp.minimum(z, 0.0) - jnp.log1p(jnp.exp(-jnp.abs(z)))
    pos = lax.broadcasted_iota(jnp.int32, z.shape, 0) % M_CHUNK
    csum = lf
    d = 1
    while d < M_CHUNK:
        csum = csum + jnp.where(pos >= d, pltpu.roll(csum, d, 0), 0.0)
        d *= 2
    is_f = (lane >= M_HEADS) & (lane < 2 * M_HEADS)
    col = jnp.where(is_f, csum, z)
    gcol_ref[...] = col
    tr = col.T
    grow_ref[...] = tr[0:M_HEADS, :] - tr[M_HEADS:2 * M_HEADS, :]


def _gates(gpre, bias):
    T = gpre.shape[0]
    return pl.pallas_call(
        _gates_kernel,
        grid=(T // GATE_ROWS,),
        in_specs=[
            pl.BlockSpec((GATE_ROWS, LANES), lambda i: (i, 0)),
            pl.BlockSpec((1, LANES), lambda i: (0, 0)),
        ],
        out_specs=[
            pl.BlockSpec((GATE_ROWS, LANES), lambda i: (i, 0)),
            pl.BlockSpec((M_HEADS, GATE_ROWS), lambda i: (0, i)),
        ],
        out_shape=[
            jax.ShapeDtypeStruct((T, LANES), _F32),
            jax.ShapeDtypeStruct((M_HEADS, T), _F32),
        ],
        compiler_params=_params(("parallel",)),
        name="gates",
    )(gpre, bias)


def _mlstm_kernel(q_ref, k_ref, v_ref, o_ref, cwq_ref, cbq_ref, cwk_ref, cbk_ref,
                  gcol_ref, grow_ref, nw_ref, *rest, n_side):
    side_in, out_ref, side_out = rest[:n_side], rest[n_side], rest[n_side + 1:2 * n_side + 1]
    xq_scr, xk_scr, c_scr, m_scr = rest[2 * n_side + 1:]
    for w_ref, w_o in zip(side_in, side_out):
        w_o[...] = w_ref[...].astype(w_o.dtype)
    head = pl.program_id(1)
    S = q_ref.shape[0]
    L = M_CHUNK
    pad = SUBLANES
    zpad = jnp.zeros((pad, M_QK_DIM), _F32)
    xq_scr[0:pad, :] = zpad
    xk_scr[0:pad, :] = zpad
    xq_scr[pad:, :] = q_ref[...].astype(_F32)
    xk_scr[pad:, :] = k_ref[...].astype(_F32)
    c_scr[...] = jnp.zeros(c_scr.shape, _F32)
    m_scr[...] = jnp.zeros(m_scr.shape, _F32)

    lane = lax.broadcasted_iota(jnp.int32, (L, LANES), 1)
    row_i = lax.broadcasted_iota(jnp.int32, (L, L), 0)
    col_i = lax.broadcasted_iota(jnp.int32, (L, L), 1)
    tril = col_i <= row_i
    ones_aug = jnp.ones((L, LANES), _BF16)
    k_scale = M_QK_DIM ** -0.5

    def conv_silu(x_scr, cw_ref, cb_ref, r0):
        acc = cb_ref[...] + cw_ref[CONV_K - 1:CONV_K, :] * x_scr[pl.ds(r0 + pad, L), :]
        for j in range(CONV_K - 1):
            acc = acc + cw_ref[j:j + 1, :] * x_scr[pl.ds(r0 + pad - (CONV_K - 1 - j), L), :]
        return acc * _sigmoid(acc)

    def chunk(c, carry):
        r0 = pl.multiple_of(c * L, L)
        qf = conv_silu(xq_scr, cwq_ref, cbq_ref, r0)
        kf = conv_silu(xk_scr, cwk_ref, cbk_ref, r0) * k_scale
        qb = qf.astype(_BF16)
        kb = kf.astype(_BF16)
        vaug = jnp.concatenate([v_ref[pl.ds(r0, L), :], ones_aug], axis=1)

        g = gcol_ref[pl.ds(r0, L), :]
        ig = jnp.sum(jnp.where(lane == head, g, 0.0), axis=-1, keepdims=True)
        b = jnp.sum(jnp.where(lane == head + M_HEADS, g, 0.0), axis=-1, keepdims=True)
        r = grow_ref[pl.ds(head, 1), pl.ds(r0, L)]
        m_prev = m_scr[...]

        log_d = jnp.where(tril, b + r, -jnp.inf)
        log_inter = b + m_prev
        m_row = jnp.maximum(log_inter, jnp.max(log_d, axis=-1, keepdims=True))
        dmat = jnp.exp(log_d - m_row)
        inter = jnp.exp(log_inter - m_row)
        s = _dot_nt(qb, kb) * dmat
        num_aug = _dot(s.astype(_BF16), vaug) + inter * _dot(qb, c_scr[...].astype(_BF16))
        num = num_aug[:, :M_V_DIM]
        den = num_aug[:, M_V_DIM:M_V_DIM + 1]
        hh = num / jnp.maximum(jnp.abs(den), jnp.exp(-m_row))
        y = _rmsnorm_rows(hh, nw_ref[...])
        gate = _sigmoid(o_ref[pl.ds(r0, L), :].astype(_F32))
        out_ref[pl.ds(r0, L), :] = (gate * y).astype(out_ref.dtype)

        b_last = b[L - 1:L, :]
        log_w = b_last - b + ig
        m_new = jnp.maximum(b_last + m_prev, jnp.max(log_w, axis=0, keepdims=True))
        w = jnp.exp(log_w - m_new)
        decay = jnp.exp(b_last + m_prev - m_new)
        kw = (kf * w).astype(_BF16)
        c_scr[...] = decay * c_scr[...] + _dot_tn(kw, vaug)
        m_scr[...] = m_new
        return carry

    lax.fori_loop(0, S // L, chunk, 0)


def _mlstm(proj3, conv_w, conv_b, gcol3, grow, nw, w_cast, layer):
    B, S, _ = proj3.shape
    steps = B * M_HEADS
    side_in, side_out, side_shape = [], [], []
    for w in w_cast:
        _, wk, wn = w.shape
        slab = wk // steps
        side_in.append(pl.BlockSpec((None, slab, wn), lambda b, h: (layer, b * M_HEADS + h, 0)))
        side_out.append(pl.BlockSpec((slab, wn), lambda b, h: (b * M_HEADS + h, 0)))
        side_shape.append(jax.ShapeDtypeStruct((wk, wn), _BF16))
    qk_blk = lambda off: (lambda b, h: (b, 0, off // M_QK_DIM + h))
    v_blk = lambda off: (lambda b, h: (b, 0, off // M_V_DIM + h))
    return pl.pallas_call(
        functools.partial(_mlstm_kernel, n_side=len(w_cast)),
        grid=(B, M_HEADS),
        in_specs=[
            pl.BlockSpec((None, S, M_QK_DIM), qk_blk(OFF_MQ)),
            pl.BlockSpec((None, S, M_QK_DIM), qk_blk(OFF_MK)),
            pl.BlockSpec((None, S, M_V_DIM), v_blk(OFF_MV)),
            pl.BlockSpec((None, S, M_V_DIM), v_blk(OFF_MO)),
            pl.BlockSpec((CONV_K, M_QK_DIM), lambda b, h: (0, h)),
            pl.BlockSpec((1, M_QK_DIM), lambda b, h: (0, h)),
            pl.BlockSpec((CONV_K, M_QK_DIM), lambda b, h: (0, M_HEADS + h)),
            pl.BlockSpec((1, M_QK_DIM), lambda b, h: (0, M_HEADS + h)),
            pl.BlockSpec((None, S, LANES), lambda b, h: (b, 0, 0)),
            pl.BlockSpec((M_HEADS, S), lambda b, h: (0, b)),
            pl.BlockSpec((1, M_V_DIM), lambda b, h: (0, h)),
        ] + side_in,
        out_specs=[pl.BlockSpec((None, S, M_V_DIM), lambda b, h: (b, 0, h))] + side_out,
        out_shape=[jax.ShapeDtypeStruct((B, S, M_V), _BF16)] + side_shape,
        scratch_shapes=[
            pltpu.VMEM((S + SUBLANES, M_QK_DIM), _F32),
            pltpu.VMEM((S + SUBLANES, M_QK_DIM), _F32),
            pltpu.VMEM((M_QK_DIM, M_V_DIM + LANES), _F32),
            pltpu.VMEM((1, 1), _F32),
        ],
        compiler_params=_params(("parallel", "parallel")),
        name="mlstm",
    )(proj3, proj3, proj3, proj3, conv_w, conv_b, conv_w, conv_b, gcol3, grow, nw, *w_cast)


def _attn_kernel(lq1_ref, lk1_ref, lq2_ref, lk2_ref, q_ref, k_ref, v_ref, nw_ref,
                 *rest, n_side, lam_init, tq, tk, sub):
    side_in, o_ref, side_out = rest[:n_side], rest[n_side], rest[n_side + 1:2 * n_side + 1]
    qs_scr, acc1, acc2, m1, l1, m2, l2 = rest[2 * n_side + 1:]
    for w_ref, w_o in zip(side_in, side_out):
        w_o[...] = (w_ref[0] if len(w_ref.shape) == 3 else w_ref[...]).astype(w_o.dtype)
    qi = pl.program_id(2)
    d = D_QK_DIM
    qs_scr[...] = (q_ref[...].astype(_F32) * ((d ** -0.5) * math.log2(math.e))).astype(_BF16)
    n_lane_blk = tk // LANES
    kv_per_q = tq // tk
    streams = ((acc1, m1, l1), (acc2, m2, l2))
    for acc, m, l in streams:
        acc[...] = jnp.zeros(acc.shape, _F32)
        m[...] = jnp.full(m.shape, -jnp.inf, _F32)
        l[...] = jnp.zeros(l.shape, _F32)

    row_i = lax.broadcasted_iota(jnp.int32, (sub, LANES), 0)
    col_i = lax.broadcasted_iota(jnp.int32, (sub, LANES), 1)

    def step(j, diag):
        r0 = pl.multiple_of(j * tk, tk)
        for idx, (acc, m, l) in enumerate(streams):
            kk = k_ref[pl.ds(r0, tk), idx * d:(idx + 1) * d]
            vv = v_ref[pl.ds(r0, tk), :]
            for r in range(tq // sub):
                if diag is not None and (r + 1) * sub - 1 < diag * tk:
                    continue
                rows = pl.ds(r * sub, sub)
                s = _dot_nt(qs_scr[rows, idx * d:(idx + 1) * d], kk)
                blocks = [s[:, b * LANES:(b + 1) * LANES] for b in range(n_lane_blk)]
                if diag is not None:
                    blocks = [blk if diag * tk + (b + 1) * LANES - 1 <= r * sub else
                              jnp.where(col_i + (diag * tk + b * LANES) <= row_i + r * sub, blk, -jnp.inf)
                              for b, blk in enumerate(blocks)]
                m_cur = blocks[0]
                for blk in blocks[1:]:
                    m_cur = jnp.maximum(m_cur, blk)
                m_prev = m[rows, :]
                m_new = jnp.maximum(m_prev, jnp.max(m_cur, axis=-1, keepdims=True))
                alpha = jnp.exp2(m_prev - m_new)
                ps = [jnp.exp2(blk - m_new) for blk in blocks]
                l_add = ps[0]
                for pb in ps[1:]:
                    l_add = l_add + pb
                l[rows, :] = alpha * l[rows, :] + l_add
                p = jnp.concatenate([pb.astype(_BF16) for pb in ps], axis=1)
                alpha2 = jnp.concatenate([alpha] * (D_V_DIM // LANES), axis=1)
                acc[rows, :] = alpha2 * acc[rows, :] + _dot(p, vv)
                m[rows, :] = m_new

    def body(j, carry):
        step(j, None)
        return carry

    lax.fori_loop(0, qi * kv_per_q, body, 0)
    for t in range(kv_per_q):
        step(qi * kv_per_q + t, t)

    lam = (jnp.exp(jnp.sum(lq1_ref[...] * lk1_ref[...], axis=-1, keepdims=True))
           - jnp.exp(jnp.sum(lq2_ref[...] * lk2_ref[...], axis=-1, keepdims=True)) + lam_init)
    o1 = acc1[...] / jnp.sum(l1[...], axis=-1, keepdims=True)
    o2 = acc2[...] / jnp.sum(l2[...], axis=-1, keepdims=True)
    hd = o1 - lam * o2
    y = _rmsnorm_rows(hd, nw_ref[...]) * (1.0 - lam_init)
    o_ref[...] = y.astype(o_ref.dtype)


def _attn(proj3, lq1, lk1, lq2, lk2, nw, lam_init, w_cast, layer, w_in_t=None, tq=2048, tk=512, sub=128):
    B, S, _ = proj3.shape
    w2 = 2 * D_QK_DIM
    n_q = S // tq
    steps = B * D_HEADS * n_q
    slab = D_MODEL // steps
    lam_spec = pl.BlockSpec((1, D_QK_DIM), lambda b, h, i: (0, 0))
    step_id = lambda b, h, i: (b * D_HEADS + h) * n_q + i
    w_in_spec = pl.BlockSpec((None, slab, D_MODEL), lambda b, h, i: (layer, step_id(b, h, i), 0))
    w_out_spec = pl.BlockSpec((slab, D_MODEL), lambda b, h, i: (step_id(b, h, i), 0))
    w_out_shape = jax.ShapeDtypeStruct((D_MODEL, D_MODEL), _BF16)
    side_in = [w_in_spec] * len(w_cast)
    side_out = [w_out_spec] * len(w_cast)
    side_shape = [w_out_shape] * len(w_cast)
    side_args = list(w_cast)
    if w_in_t is not None:
        rows = N_PROJ // steps
        el = pl.Element
        side_in.append(pl.BlockSpec((el(1), el(rows), el(D_MODEL)),
                                    lambda b, h, i: (layer + 1, _proj_rows(step_id(b, h, i), rows), 0)))
        side_out.append(pl.BlockSpec((rows, D_MODEL), lambda b, h, i: (step_id(b, h, i), 0)))
        side_shape.append(jax.ShapeDtypeStruct((N_PROJ, D_MODEL), _BF16))
        side_args.append(w_in_t)
    return pl.pallas_call(
        functools.partial(_attn_kernel, n_side=len(side_args), lam_init=lam_init, tq=tq, tk=tk, sub=sub),
        grid=(B, D_HEADS, n_q),
        in_specs=[
            lam_spec, lam_spec, lam_spec, lam_spec,
            pl.BlockSpec((None, tq, w2), lambda b, h, i: (b, i, OFF_DQ // w2 + h)),
            pl.BlockSpec((None, S, w2), lambda b, h, i: (b, 0, OFF_DK // w2 + h)),
            pl.BlockSpec((None, S, D_V_DIM), lambda b, h, i: (b, 0, OFF_DV // D_V_DIM + h)),
            pl.BlockSpec((1, D_V_DIM), lambda b, h, i: (0, h)),
        ] + side_in,
        out_specs=[pl.BlockSpec((None, tq, D_V_DIM), lambda b, h, i: (b, i, h))] + side_out,
        out_shape=[jax.ShapeDtypeStruct((B, S, D_V), _BF16)] + side_shape,
        scratch_shapes=[
            pltpu.VMEM((tq, w2), _BF16),
            pltpu.VMEM((tq, D_V_DIM), _F32),
            pltpu.VMEM((tq, D_V_DIM), _F32),
            pltpu.VMEM((tq, LANES), _F32),
            pltpu.VMEM((tq, LANES), _F32),
            pltpu.VMEM((tq, LANES), _F32),
            pltpu.VMEM((tq, LANES), _F32),
        ],
        compiler_params=_params(("parallel", "parallel", "arbitrary")),
        name="diffattn",
    )(lq1, lk1, lq2, lk2, proj3, proj3, proj3, nw, *side_args)


def _merge_kernel(hm_ref, hd_ref, wm_ref, wd_ref, gm_ref, gd_ref, y_ref):
    a = _dot(hm_ref[...], wm_ref[...])
    b = _dot(hd_ref[...], wd_ref[...])
    y = _sigmoid(gm_ref[...].astype(_F32)) * a + _sigmoid(gd_ref[...].astype(_F32)) * b
    y_ref[...] = y.astype(y_ref.dtype)


def _merge(hm, hd, wm, wd, proj, tm=1024, tn=1024):
    T = hm.shape[0]
    return pl.pallas_call(
        _merge_kernel,
        grid=(T // tm, D_MODEL // tn),
        in_specs=[
            pl.BlockSpec((tm, M_V), lambda i, j: (i, 0)),
            pl.BlockSpec((tm, D_V), lambda i, j: (i, 0)),
            pl.BlockSpec((M_V, tn), lambda i, j: (0, j)),
            pl.BlockSpec((D_V, tn), lambda i, j: (0, j)),
            pl.BlockSpec((tm, tn), lambda i, j: (i, OFF_GM // tn + j)),
            pl.BlockSpec((tm, tn), lambda i, j: (i, OFF_GD // tn + j)),
        ],
        out_specs=pl.BlockSpec((tm, tn), lambda i, j: (i, j)),
        out_shape=jax.ShapeDtypeStruct((T, D_MODEL), _BF16),
        compiler_params=_params(("parallel", "parallel")),
        name="merge",
    )(hm, hd, wm, wd, proj, proj)


def _resmm_kernel(a_ref, w_ref, x_ref, o_ref):
    o_ref[...] = x_ref[...] + _dot(a_ref[...], w_ref[...])


def _resmm(a, w, x, tm, tn, name):
    T, K = a.shape
    N = w.shape[1]
    return pl.pallas_call(
        _resmm_kernel,
        grid=(T // tm, N // tn),
        in_specs=[
            pl.BlockSpec((tm, K), lambda i, j: (i, 0)),
            pl.BlockSpec((K, tn), lambda i, j: (0, j)),
            pl.BlockSpec((tm, tn), lambda i, j: (i, j)),
        ],
        out_specs=pl.BlockSpec((tm, tn), lambda i, j: (i, j)),
        out_shape=jax.ShapeDtypeStruct((T, N), _F32),
        compiler_params=_params(("parallel", "parallel")),
        name=name,
    )(a, w, x)


def _ffn_up_kernel(x_ref, nw_ref, wg_ref, wu_ref, a_ref, h_scr):
    @pl.when(pl.program_id(1) == 0)
    def _():
        h_scr[...] = _rmsnorm_rows(x_ref[...], nw_ref[...]).astype(_BF16)

    h = h_scr[...]
    g = _dot(h, wg_ref[...])
    u = _dot(h, wu_ref[...])
    a_ref[...] = (g * _sigmoid(g) * u).astype(a_ref.dtype)


def _ffn_up(x, nw, wg, wu, tm=1024, tn=512):
    T = x.shape[0]
    return pl.pallas_call(
        _ffn_up_kernel,
        grid=(T // tm, D_FF // tn),
        in_specs=[
            pl.BlockSpec((tm, D_MODEL), lambda i, j: (i, 0)),
            pl.BlockSpec((1, D_MODEL), lambda i, j: (0, 0)),
            pl.BlockSpec((D_MODEL, tn), lambda i, j: (0, j)),
            pl.BlockSpec((D_MODEL, tn), lambda i, j: (0, j)),
        ],
        out_specs=pl.BlockSpec((tm, tn), lambda i, j: (i, j)),
        out_shape=jax.ShapeDtypeStruct((T, D_FF), _BF16),
        scratch_shapes=[pltpu.VMEM((tm, D_MODEL), _BF16)],
        compiler_params=_params(("parallel", "arbitrary")),
        name="ffn_up",
    )(x, nw, wg, wu)


def _norm_kernel(x_ref, nw_ref, o_ref):
    o_ref[...] = _rmsnorm_rows(x_ref[...], nw_ref[...])


def _final_norm(x, nw, tm=512):
    T = x.shape[0]
    return pl.pallas_call(
        _norm_kernel,
        grid=(T // tm,),
        in_specs=[
            pl.BlockSpec((tm, D_MODEL), lambda i: (i, 0)),
            pl.BlockSpec((1, D_MODEL), lambda i: (0, 0)),
        ],
        out_specs=pl.BlockSpec((tm, D_MODEL), lambda i: (i, 0)),
        out_shape=jax.ShapeDtypeStruct((T, D_MODEL), _F32),
        compiler_params=_params(("parallel",)),
        name="final_norm",
    )(x, nw)


def kernel(x, attn_norm_w, w_in, conv_w, conv_b, b_igate, b_fgate, mlstm_norm_w, lambda_q1, lambda_k1, lambda_q2, lambda_k2, diff_norm_w, w_branch_m, w_branch_d, w_out, ffn_norm_w, w_ffn_gate, w_ffn_up, w_ffn_down, final_norm_w):
    B, S, D = x.shape
    T = B * S
    xf = x.reshape(T, D)
    gate_pad = jnp.zeros((LANES - 2 * M_HEADS,), _F32)
    w_in_t = jnp.swapaxes(w_in, 1, 2)
    w_proj = None
    for l in range(DEPTH):
        lam_init = 0.8 - 0.6 * math.exp(-0.3 * l)
        gate_bias = jnp.concatenate([b_igate[l], b_fgate[l], gate_pad]).reshape(1, LANES)

        proj, gpre = _inproj(xf, attn_norm_w[l].reshape(1, D), w_in_t, l, w_proj)
        gcol, grow = _gates(gpre, gate_bias)
        proj3 = proj.reshape(B, S, N_PROJ)
        hm, w_fd, w_fg, w_fu = _mlstm(proj3, conv_w[l], conv_b[l].reshape(1, 2 * M_QK),
                                      gcol.reshape(B, S, LANES), grow, mlstm_norm_w[l].reshape(1, M_V),
                                      (w_ffn_down, w_ffn_gate, w_ffn_up), l)
        attn_out = _attn(proj3, lambda_q1[l].reshape(1, D_QK_DIM), lambda_k1[l].reshape(1, D_QK_DIM),
                         lambda_q2[l].reshape(1, D_QK_DIM), lambda_k2[l].reshape(1, D_QK_DIM),
                         diff_norm_w[l].reshape(1, D_V), lam_init, (w_branch_m, w_branch_d, w_out), l,
                         w_in_t if l + 1 < DEPTH else None)
        hd, w_bm, w_bd, w_o = attn_out[:4]
        w_proj = attn_out[4] if l + 1 < DEPTH else None
        y = _merge(hm.reshape(T, M_V), hd.reshape(T, D_V), w_bm, w_bd, proj)
        xf = _resmm(y, w_o, xf, 512, D_MODEL, "out_proj")
        a = _ffn_up(xf, ffn_norm_w[l].reshape(1, D), w_fg, w_fu)
        xf = _resmm(a, w_fd, xf, 1024, 512, "ffn_down")
    out = _final_norm(xf, final_norm_w.reshape(1, D))
    return out.reshape(B, S, D)
```
